```python
import jax, jax.numpy as jnp
from jax import lax
import numpy as np

D_MODEL = 1024
BATCH = 1
SEQ = 16384
DEPTH = 2
DEC_BATCH = 8
DEC_SEQ = 4096
PAST_LEN = 128

N_MEM = 256
D_CONV_A = D_MODEL
CONV_A_WIDTH = 31
D_CONV_B = D_MODEL
CONV_B_WIDTH = 3
N_XHEADS = 4
D_XATTN = D_MODEL
XHEAD_DIM = D_XATTN // N_XHEADS
N_BRANCH = 3
D_FF = 2816
EPS = 1e-6
D_IN = 2 * D_CONV_A + 3 * D_CONV_B + D_XATTN + N_BRANCH * D_MODEL

kernel_name = "macaron_parallel_conv_xattn_encoder"


def rmsnorm(x, g):
    xf = x.astype(jnp.float32)
    y = xf * lax.rsqrt(jnp.mean(xf * xf, axis=-1, keepdims=True) + EPS)
    return (y * g.astype(jnp.float32)).astype(x.dtype)


def layernorm(x, g, b):
    xf = x.astype(jnp.float32)
    mu = jnp.mean(xf, axis=-1, keepdims=True)
    xc = xf - mu
    var = jnp.mean(xc * xc, axis=-1, keepdims=True)
    y = xc * lax.rsqrt(var + EPS) * g.astype(jnp.float32) + b.astype(jnp.float32)
    return y.astype(x.dtype)


def swiglu(x, w_gate, w_up, w_down):
    return (jax.nn.silu(x @ w_gate) * (x @ w_up)) @ w_down


def depthwise_conv(x, w):
    pad = w.shape[0] // 2
    return lax.conv_general_dilated(
        x, w[:, None, :], window_strides=(1,), padding=[(pad, pad)],
        dimension_numbers=("NWC", "WIO", "NWC"), feature_group_count=x.shape[-1])


def cross_attention(q, mem_n, w_kv, w_out):
    b, s, _ = q.shape
    m = mem_n.shape[1]
    k, v = jnp.split(mem_n @ w_kv, 2, axis=-1)
    q = q.reshape(b, s, N_XHEADS, XHEAD_DIM)
    k = k.reshape(b, m, N_XHEADS, XHEAD_DIM)
    v = v.reshape(b, m, N_XHEADS, XHEAD_DIM)
    scores = jnp.einsum("bshd,bmhd->bhsm", q, k).astype(jnp.float32) * (XHEAD_DIM ** -0.5)
    probs = jax.nn.softmax(scores, axis=-1).astype(v.dtype)
    o = jnp.einsum("bhsm,bmhd->bshd", probs, v).reshape(b, s, D_XATTN)
    return o @ w_out


def encoder_layer(x, mem, p):
    (ffn1_norm, ffn1_wg, ffn1_wu, ffn1_wd, mix_norm, mem_norm, w_in,
     conv_a_w, conv_a_b, ln_a_g, ln_a_b, w_a_out, conv_b_w, w_b_out,
     w_kv, w_x_out, w_o, ffn2_norm, ffn2_wg, ffn2_wu, ffn2_wd) = p
    b, s, _ = x.shape
    x = x + 0.5 * swiglu(rmsnorm(x, ffn1_norm), ffn1_wg, ffn1_wu, ffn1_wd)
    u = rmsnorm(x, mix_norm)
    z = u @ w_in
    splits = [D_CONV_A, 2 * D_CONV_A, 2 * D_CONV_A + D_CONV_B, 2 * D_CONV_A + 2 * D_CONV_B,
              2 * D_CONV_A + 3 * D_CONV_B, 2 * D_CONV_A + 3 * D_CONV_B + D_XATTN]
    a_val, a_gate, b_x, b_gate_b, b_gate_c, q, gates = jnp.split(z, splits, axis=-1)
    a = a_val * jax.nn.sigmoid(a_gate)
    a = depthwise_conv(a, conv_a_w) + conv_a_b
    a = jax.nn.silu(layernorm(a, ln_a_g, ln_a_b))
    y_a = a @ w_a_out
    y_b = (b_gate_b * depthwise_conv(b_gate_c * b_x, conv_b_w)) @ w_b_out
    y_c = cross_attention(q, rmsnorm(mem, mem_norm), w_kv, w_x_out)
    g = jax.nn.sigmoid(gates).reshape(b, s, N_BRANCH, D_MODEL)
    merged = g[:, :, 0, :] * y_a + g[:, :, 1, :] * y_b + g[:, :, 2, :] * y_c
    x = x + merged @ w_o
    x = x + 0.5 * swiglu(rmsnorm(x, ffn2_norm), ffn2_wg, ffn2_wu, ffn2_wd)
    return x


def encoder_trunk(x, mem, layer_params, final_norm):
    for l in range(DEPTH):
        x = encoder_layer(x, mem, tuple(w[l] for w in layer_params))
    return rmsnorm(x, final_norm)


def setup_inputs(seed: int = 0) -> dict:
    key = jax.random.key(seed)
    ks = iter(jax.random.split(key, 32))

    def dense(shape, fan_in):
        return jax.random.normal(next(ks), shape, jnp.float32) * (fan_in ** -0.5)

    def gain(shape):
        return 1.0 + 0.01 * jax.random.normal(next(ks), shape, jnp.float32)

    def small(shape):
        return 0.01 * jax.random.normal(next(ks), shape, jnp.float32)

    L, D = DEPTH, D_MODEL
    return {
        "x_prompt": jax.random.normal(next(ks), (BATCH, SEQ, D), jnp.float32),
        "x_sample": jax.random.normal(next(ks), (DEC_BATCH, DEC_SEQ, D), jnp.float32),
        "mem_prompt": jax.random.normal(next(ks), (BATCH, N_MEM, D), jnp.float32),
        "mem_sample": jax.random.normal(next(ks), (DEC_BATCH, N_MEM, D), jnp.float32),
        "ffn1_norm": gain((L, D)),
        "ffn1_wg": dense((L, D, D_FF), D),
        "ffn1_wu": dense((L, D, D_FF), D),
        "ffn1_wd": dense((L, D_FF, D), D_FF),
        "mix_norm": gain((L, D)),
        "mem_norm": gain((L, D)),
        "w_in": dense((L, D, D_IN), D),
        "conv_a_w": dense((L, CONV_A_WIDTH, D_CONV_A), CONV_A_WIDTH),
        "conv_a_b": small((L, D_CONV_A)),
        "ln_a_g": gain((L, D_CONV_A)),
        "ln_a_b": small((L, D_CONV_A)),
        "w_a_out": dense((L, D_CONV_A, D), D_CONV_A),
        "conv_b_w": dense((L, CONV_B_WIDTH, D_CONV_B), CONV_B_WIDTH),
        "w_b_out": dense((L, D_CONV_B, D), D_CONV_B),
        "w_kv": dense((L, D, 2 * D_XATTN), D),
        "w_x_out": dense((L, D_XATTN, D), D_XATTN),
        "w_o": dense((L, D, D), D),
        "ffn2_norm": gain((L, D)),
        "ffn2_wg": dense((L, D, D_FF), D),
        "ffn2_wu": dense((L, D, D_FF), D),
        "ffn2_wd": dense((L, D_FF, D), D_FF),
        "final_norm": gain((D,)),
    }


def reference(x_prompt, x_sample, mem_prompt, mem_sample,
              ffn1_norm, ffn1_wg, ffn1_wu, ffn1_wd, mix_norm, mem_norm, w_in,
              conv_a_w, conv_a_b, ln_a_g, ln_a_b, w_a_out, conv_b_w, w_b_out,
              w_kv, w_x_out, w_o, ffn2_norm, ffn2_wg, ffn2_wu, ffn2_wd, final_norm):
    layer_params = (ffn1_norm, ffn1_wg, ffn1_wu, ffn1_wd, mix_norm, mem_norm, w_in,
                    conv_a_w, conv_a_b, ln_a_g, ln_a_b, w_a_out, conv_b_w, w_b_out,
                    w_kv, w_x_out, w_o, ffn2_norm, ffn2_wg, ffn2_wu, ffn2_wd)
    y_prompt = encoder_trunk(x_prompt, mem_prompt, layer_params, final_norm)
    y_sample = encoder_trunk(x_sample, mem_sample, layer_params, final_norm)
    return (y_prompt, y_sample)
```

```python
import functools

import jax
import jax.numpy as jnp
from jax import lax
from jax.experimental import pallas as pl
from jax.experimental.pallas import tpu as pltpu

D_MODEL = 1024
D_FF = 2816
N_MEM = 256
N_HEADS = 4
HEAD_DIM = D_MODEL // N_HEADS
CONV_A_WIDTH = 31
CONV_B_WIDTH = 3
EPS = 1e-6

HALO = 16
FFN_TILE = 512
MIX_TILE = 256
VMEM_LIMIT_BYTES = 56 * 1024 * 1024

F32 = jnp.float32
BF16 = jnp.bfloat16


def _rmsnorm(x, g):
    return x * lax.rsqrt(jnp.mean(x * x, axis=-1, keepdims=True) + EPS) * g


def _dot(a, b):
    return jnp.dot(a, b, preferred_element_type=F32)


def _resident(shape):
    return pl.BlockSpec(shape, lambda i: (0,) * len(shape), pipeline_mode=pl.Buffered(1))


def _kv_kernel(mem_ref, g_ref, wkv_ref, k_ref, v_ref):
    mem_n = _rmsnorm(mem_ref[...], g_ref[...]).astype(BF16)
    kv = _dot(mem_n, wkv_ref[...])
    k_ref[...] = (kv[:, :D_MODEL] * (HEAD_DIM ** -0.5)).astype(BF16)
    v_ref[...] = kv[:, D_MODEL:].astype(BF16)


def _kv_call(mem, g, wkv):
    b = mem.shape[0]
    out = jax.ShapeDtypeStruct((b, N_MEM, D_MODEL), BF16)
    blk = pl.BlockSpec((None, N_MEM, D_MODEL), lambda i: (i, 0, 0))
    return pl.pallas_call(
        _kv_kernel,
        grid=(b,),
        in_specs=[blk, _resident((1, D_MODEL)), _resident((D_MODEL, 2 * D_MODEL))],
        out_specs=[blk, blk],
        out_shape=[out, out],
        compiler_params=pltpu.CompilerParams(
            dimension_semantics=("arbitrary",), vmem_limit_bytes=VMEM_LIMIT_BYTES),
        name="kv_proj",
    )(mem, g, wkv)


def _ffn_kernel(x_ref, g_ref, wg_ref, wu_ref, wd_ref, *rest, final_norm):
    o_ref = rest[-1]
    x = x_ref[...]
    h = _rmsnorm(x, g_ref[...]).astype(BF16)
    gate = _dot(h, wg_ref[...])
    up = _dot(h, wu_ref[...])
    act = (jax.nn.silu(gate) * up).astype(BF16)
    y = x + 0.5 * _dot(act, wd_ref[...])
    if final_norm:
        y = _rmsnorm(y, rest[0][...])
    o_ref[...] = y


def _ffn_call(x, g, wg, wu, wd, final_g=None):
    rows = x.shape[0]
    tile = pl.BlockSpec((FFN_TILE, D_MODEL), lambda i: (i, 0))
    in_specs = [tile, _resident((1, D_MODEL)), _resident((D_MODEL, D_FF)),
                _resident((D_MODEL, D_FF)), _resident((D_FF, D_MODEL))]
    args = [x, g, wg, wu, wd]
    if final_g is not None:
        in_specs.append(_resident((1, D_MODEL)))
        args.append(final_g)
    return pl.pallas_call(
        functools.partial(_ffn_kernel, final_norm=final_g is not None),
        grid=(rows // FFN_TILE,),
        in_specs=in_specs,
        out_specs=tile,
        out_shape=jax.ShapeDtypeStruct((rows, D_MODEL), F32),
        compiler_params=pltpu.CompilerParams(
            dimension_semantics=("arbitrary",), vmem_limit_bytes=VMEM_LIMIT_BYTES),
        name="ffn_final" if final_g is not None else "ffn",
    )(*args)


def _mixer_kernel(xm_ref, xp_ref, xn_ref, k_ref, v_ref, g_ref, win_ref,
                  caw_ref, cab_ref, lng_ref, lnb_ref, wao_ref, cbw_ref, wbo_ref,
                  wxo_ref, wo_ref, o_ref, u_scr, sa_scr, sb_scr, *, tiles_per_seq):
    tm = xm_ref.shape[0]
    ext = tm + 2 * HALO
    main = slice(HALO, HALO + tm)
    col = lambda j: slice(j * D_MODEL, (j + 1) * D_MODEL)

    i = pl.program_id(0)
    first = (i % tiles_per_seq) == 0
    last = (i % tiles_per_seq) == tiles_per_seq - 1

    g = g_ref[...]
    u_scr[0:HALO, :] = _rmsnorm(xp_ref[...], g).astype(BF16)
    u_scr[main, :] = _rmsnorm(xm_ref[...], g).astype(BF16)
    u_scr[HALO + tm:ext, :] = _rmsnorm(xn_ref[...], g).astype(BF16)
    u_ext = u_scr[...]
    u_main = u_scr[main, :]

    row = lax.broadcasted_iota(jnp.int32, (ext, 1), 0)
    inside = jnp.logical_and(jnp.logical_or(row >= HALO, jnp.logical_not(first)),
                             jnp.logical_or(row < HALO + tm, jnp.logical_not(last)))

    za = _dot(u_ext, win_ref[:, 0:2 * D_MODEL])
    a_in = za[:, col(0)] * jax.nn.sigmoid(za[:, col(1)])
    sa_scr[...] = jnp.where(inside, a_in, 0.0)
    a = jnp.broadcast_to(cab_ref[...], (tm, D_MODEL))
    for t in range(CONV_A_WIDTH):
        a = a + caw_ref[t:t + 1, :] * sa_scr[pl.ds(HALO - CONV_A_WIDTH // 2 + t, tm), :]
    mu = jnp.mean(a, axis=-1, keepdims=True)
    ac = a - mu
    var = jnp.mean(ac * ac, axis=-1, keepdims=True)
    a = jax.nn.silu(ac * lax.rsqrt(var + EPS) * lng_ref[...] + lnb_ref[...])
    y_a = _dot(a.astype(BF16), wao_ref[...])
    merged = jax.nn.sigmoid(_dot(u_main, win_ref[:, col(6)])) * y_a

    zb = _dot(u_ext, win_ref[:, 2 * D_MODEL:5 * D_MODEL])
    sb_scr[...] = jnp.where(inside, zb[:, col(2)] * zb[:, col(0)], 0.0)
    b = cbw_ref[0:1, :] * sb_scr[pl.ds(HALO - 1, tm), :]
    for t in range(1, CONV_B_WIDTH):
        b = b + cbw_ref[t:t + 1, :] * sb_scr[pl.ds(HALO - 1 + t, tm), :]
    y_b = _dot((zb[main, col(1)] * b).astype(BF16), wbo_ref[...])
    merged = merged + jax.nn.sigmoid(_dot(u_main, win_ref[:, col(7)])) * y_b

    q = _dot(u_main, win_ref[:, col(5)]).astype(BF16)
    heads = []
    for h in range(N_HEADS):
        hs = slice(h * HEAD_DIM, (h + 1) * HEAD_DIM)
        s = lax.dot_general(q[:, hs], k_ref[:, hs], (((1,), (1,)), ((), ())),
                            preferred_element_type=F32)
        p = jnp.exp(s - jnp.max(s, axis=-1, keepdims=True))
        denom = jnp.sum(p, axis=-1, keepdims=True)
        heads.append(_dot(p.astype(BF16), v_ref[:, hs]) / denom)
    o = jnp.concatenate(heads, axis=-1).astype(BF16)
    y_c = _dot(o, wxo_ref[...])
    merged = merged + jax.nn.sigmoid(_dot(u_main, win_ref[:, col(8)])) * y_c

    o_ref[...] = xm_ref[...] + _dot(merged.astype(BF16), wo_ref[...])


def _mixer_call(x, k, v, seq_len, g, win, caw, cab, lng, lnb, wao, cbw, wbo, wxo, wo):
    rows = x.shape[0]
    tm = MIX_TILE
    tiles_per_seq = seq_len // tm
    halo_per_tile = tm // HALO
    last_halo_block = rows // HALO - 1
    d_in = win.shape[1]
    ext = tm + 2 * HALO

    tile = pl.BlockSpec((tm, D_MODEL), lambda i: (i, 0))
    prev = pl.BlockSpec((HALO, D_MODEL),
                        lambda i: (jnp.maximum(i * halo_per_tile - 1, 0), 0))
    nxt = pl.BlockSpec((HALO, D_MODEL),
                       lambda i: (jnp.minimum((i + 1) * halo_per_tile, last_halo_block), 0))
    kv_blk = pl.BlockSpec((None, N_MEM, D_MODEL), lambda i: (i // tiles_per_seq, 0, 0))
    sq = _resident((D_MODEL, D_MODEL))
    vec = _resident((1, D_MODEL))
    return pl.pallas_call(
        functools.partial(_mixer_kernel, tiles_per_seq=tiles_per_seq),
        grid=(rows // tm,),
        in_specs=[tile, prev, nxt, kv_blk, kv_blk, vec, _resident((D_MODEL, d_in)),
                  _resident((CONV_A_WIDTH, D_MODEL)), vec, vec, vec, sq,
                  _resident((CONV_B_WIDTH, D_MODEL)), sq, sq, sq],
        out_specs=tile,
        out_shape=jax.ShapeDtypeStruct((rows, D_MODEL), F32),
        scratch_shapes=[pltpu.VMEM((ext, D_MODEL), BF16),
                        pltpu.VMEM((ext, D_MODEL), F32),
                        pltpu.VMEM((ext, D_MODEL), F32)],
        compiler_params=pltpu.CompilerParams(
            dimension_semantics=("arbitrary",), vmem_limit_bytes=VMEM_LIMIT_BYTES),
        name="mixer",
    )(x, x, x, k, v, g, win, caw, cab, lng, lnb, wao, cbw, wbo, wxo, wo)


def _trunk(x, mem, p, final_norm):
    b, s, _ = x.shape
    depth = p["w_in"].shape[0]
    x2 = x.reshape(b * s, D_MODEL)
    row = lambda w: w.reshape(1, D_MODEL)
    for l in range(depth):
        k, v = _kv_call(mem, row(p["mem_norm"][l]), p["w_kv"][l])
        x2 = _ffn_call(x2, row(p["ffn1_norm"][l]), p["ffn1_wg"][l], p["ffn1_wu"][l],
                       p["ffn1_wd"][l])
        x2 = _mixer_call(x2, k, v, s, row(p["mix_norm"][l]), p["w_in"][l],
                         p["conv_a_w"][l], row(p["conv_a_b"][l]), row(p["ln_a_g"][l]),
                         row(p["ln_a_b"][l]), p["w_a_out"][l], p["conv_b_w"][l],
                         p["w_b_out"][l], p["w_x_out"][l], p["w_o"][l])
        x2 = _ffn_call(x2, row(p["ffn2_norm"][l]), p["ffn2_wg"][l], p["ffn2_wu"][l],
                       p["ffn2_wd"][l],
                       final_g=row(final_norm) if l == depth - 1 else None)
    return x2.reshape(b, s, D_MODEL)


_MATMUL_WEIGHTS = ("ffn1_wg", "ffn1_wu", "ffn1_wd", "w_in", "w_a_out", "w_b_out",
                   "w_kv", "w_x_out", "w_o", "ffn2_wg", "ffn2_wu", "ffn2_wd")


def kernel(x_prompt, x_sample, mem_prompt, mem_sample, ffn1_norm, ffn1_wg, ffn1_wu, ffn1_wd, mix_norm, mem_norm, w_in, conv_a_w, conv_a_b, ln_a_g, ln_a_b, w_a_out, conv_b_w, w_b_out, w_kv, w_x_out, w_o, ffn2_norm, ffn2_wg, ffn2_wu, ffn2_wd, final_norm):
    p = dict(ffn1_norm=ffn1_norm, ffn1_wg=ffn1_wg, ffn1_wu=ffn1_wu, ffn1_wd=ffn1_wd,
             mix_norm=mix_norm, mem_norm=mem_norm, w_in=w_in, conv_a_w=conv_a_w,
             conv_a_b=conv_a_b, ln_a_g=ln_a_g, ln_a_b=ln_a_b, w_a_out=w_a_out,
             conv_b_w=conv_b_w, w_b_out=w_b_out, w_kv=w_kv, w_x_out=w_x_out, w_o=w_o,
             ffn2_norm=ffn2_norm, ffn2_wg=ffn2_wg, ffn2_wu=ffn2_wu, ffn2_wd=ffn2_wd)
    for name in _MATMUL_WEIGHTS:
        p[name] = p[name].astype(BF16)
    y_prompt = _trunk(x_prompt, mem_prompt, p, final_norm)
    y_sample = _trunk(x_sample, mem_sample, p, final_norm)
    return (y_prompt, y_sample)
```

```python
import functools

import jax
import jax.numpy as jnp
from jax import lax
from jax.experimental import pallas as pl
from jax.experimental.pallas import tpu as pltpu

D_MODEL = 1024
D_FF = 2816
N_MEM = 256
N_HEADS = 4
HEAD_DIM = D_MODEL // N_HEADS
CONV_A_WIDTH = 31
CONV_B_WIDTH = 3
EPS = 1e-6

HALO = 16
SUBLANES = 8
LANES = 128
FFN_TILE = 512
MIX_TILE = 256
PROJ_CHUNK = 512
VMEM_LIMIT_BYTES = 56 * 1024 * 1024

F32 = jnp.float32
BF16 = jnp.bfloat16


def _rmsnorm(x, g):
    return x * lax.rsqrt(jnp.mean(x * x, axis=-1, keepdims=True) + EPS) * g


def _dot(a, b):
    return jnp.dot(a, b, preferred_element_type=F32)


def _resident(shape):
    return pl.BlockSpec(shape, lambda i: (0,) * len(shape), pipeline_mode=pl.Buffered(1))


def _kv_kernel(mem_ref, g_ref, wkv_ref, k_ref, v_ref):
    mem_n = _rmsnorm(mem_ref[...], g_ref[...]).astype(BF16)
    kv = _dot(mem_n, wkv_ref[:, :2 * D_MODEL])
    k_ref[...] = (kv[:, :D_MODEL] * (HEAD_DIM ** -0.5)).astype(BF16)
    v_ref[...] = kv[:, D_MODEL:].astype(BF16)


def _kv_call(mem, g, wkv):
    b = mem.shape[0]
    out = jax.ShapeDtypeStruct((b, N_MEM, D_MODEL), BF16)
    blk = pl.BlockSpec((None, N_MEM, D_MODEL), lambda i: (i, 0, 0))
    return pl.pallas_call(
        _kv_kernel,
        grid=(b,),
        in_specs=[blk, _resident((1, D_MODEL)), _resident(wkv.shape)],
        out_specs=[blk, blk],
        out_shape=[out, out],
        compiler_params=pltpu.CompilerParams(
            dimension_semantics=("arbitrary",), vmem_limit_bytes=VMEM_LIMIT_BYTES),
        name="kv_proj",
    )(mem, g, wkv)


def _ffn_kernel(x_ref, g_ref, wg_ref, wu_ref, wd_ref, *rest, final_norm):
    o_ref = rest[-1]
    x = x_ref[...]
    h = _rmsnorm(x, g_ref[...]).astype(BF16)
    gate = _dot(h, wg_ref[...])
    up = _dot(h, wu_ref[...])
    act = (jax.nn.silu(gate) * up).astype(BF16)
    y = x + 0.5 * _dot(act, wd_ref[:, :D_MODEL])
    if final_norm:
        y = _rmsnorm(y, rest[0][...])
    o_ref[...] = y


def _ffn_call(x, g, wg, wu, wd, final_g=None):
    rows = x.shape[0]
    tile = pl.BlockSpec((FFN_TILE, D_MODEL), lambda i: (i, 0))
    in_specs = [tile, _resident((1, D_MODEL)), _resident((D_MODEL, D_FF)),
                _resident((D_MODEL, D_FF)), _resident(wd.shape)]
    args = [x, g, wg, wu, wd]
    if final_g is not None:
        in_specs.append(_resident((1, D_MODEL)))
        args.append(final_g)
    return pl.pallas_call(
        functools.partial(_ffn_kernel, final_norm=final_g is not None),
        grid=(rows // FFN_TILE,),
        in_specs=in_specs,
        out_specs=tile,
        out_shape=jax.ShapeDtypeStruct((rows, D_MODEL), F32),
        compiler_params=pltpu.CompilerParams(
            dimension_semantics=("arbitrary",), vmem_limit_bytes=VMEM_LIMIT_BYTES),
        name="ffn_final" if final_g is not None else "ffn",
    )(*args)


def _conv_a_blocks(s_scr, wb_scr, bias, out_scr):
    tm = out_scr.shape[0]
    first = HALO - CONV_A_WIDTH // 2
    nq = pl.cdiv(first + CONV_A_WIDTH, SUBLANES)
    sub = lax.broadcasted_iota(jnp.int32, (SUBLANES, LANES), 0)

    for c in range(0, D_MODEL, LANES):
        lanes = slice(c, c + LANES)
        w = [wb_scr[SUBLANES * k:SUBLANES * (k + 1), lanes] for k in range(CONV_A_WIDTH)]
        bias_c = bias[:, lanes]

        def rotated_partials(j):
            s = [s_scr[j + SUBLANES * q:j + SUBLANES * (q + 1), lanes] for q in range(nq)]
            parts = []
            for r in range(SUBLANES):
                acc = None
                for q in range(nq):
                    k = SUBLANES * q + r - first
                    if 0 <= k < CONV_A_WIDTH:
                        term = w[k] * s[q]
                        acc = term if acc is None else acc + term
                parts.append(acc if r == 0 else pltpu.roll(acc, SUBLANES - r, axis=0))
            return parts

        cur = rotated_partials(0)
        for blk in range(tm // SUBLANES):
            nxt = rotated_partials(SUBLANES * (blk + 1))
            acc = cur[0] + bias_c
            for r in range(1, SUBLANES):
                acc = acc + jnp.where(sub < SUBLANES - r, cur[r], nxt[r])
            after = yield
            if after is not None:
                acc = acc + after
            out_scr[SUBLANES * blk:SUBLANES * (blk + 1), lanes] = acc
            cur = nxt


def _ordering_zero(x):
    bits = x[-SUBLANES:, -LANES:].astype(jnp.int32)
    return lax.shift_right_logical(lax.shift_right_logical(bits, 16), 16).astype(F32)


def _interleave(steps, n_steps, fillers):
    done = 0
    next(steps)
    for j in range(n_steps):
        target = ((j + 1) * len(fillers)) // n_steps
        after = None
        while done < target:
            zero = fillers[done]()
            after = zero if after is None else after + zero
            done += 1
        try:
            steps.send(after)
        except StopIteration:
            assert j == n_steps - 1
    assert done == len(fillers)


def _mixer_kernel(xm_ref, xp_ref, xn_ref, k_ref, v_ref, g_ref, win_ref,
                  caw_ref, cab_ref, lng_ref, lnb_ref, wao_ref, cbw_ref, wbo_ref,
                  wxo_ref, wo_ref, o_ref, u_scr, sa_scr, sb_scr, wb_scr, ca_scr,
                  zb_scr, zq_scr, *, tiles_per_seq):
    tm = xm_ref.shape[0]
    ext = tm + 2 * HALO
    main = slice(HALO, HALO + tm)
    col = lambda j: slice(j * D_MODEL, (j + 1) * D_MODEL)

    i = pl.program_id(0)
    first = (i % tiles_per_seq) == 0
    last = (i % tiles_per_seq) == tiles_per_seq - 1

    g = g_ref[...]
    u_scr[0:HALO, :] = _rmsnorm(xp_ref[...], g).astype(BF16)
    u_scr[main, :] = _rmsnorm(xm_ref[...], g).astype(BF16)
    u_scr[HALO + tm:ext, :] = _rmsnorm(xn_ref[...], g).astype(BF16)

    row = lax.broadcasted_iota(jnp.int32, (ext, 1), 0)
    inside = jnp.logical_and(jnp.logical_or(row >= HALO, jnp.logical_not(first)),
                             jnp.logical_or(row < HALO + tm, jnp.logical_not(last)))

    za = _dot(u_scr[...], win_ref[:, 0:2 * D_MODEL])
    a_in = za[:, col(0)] * jax.nn.sigmoid(za[:, col(1)])
    sa_scr[...] = jnp.where(inside, a_in, 0.0)
    for t in range(CONV_A_WIDTH):
        wb_scr[SUBLANES * t:SUBLANES * (t + 1), :] = jnp.broadcast_to(
            caw_ref[t:t + 1, :], (SUBLANES, D_MODEL))

    def project_b(c):
        z = _dot(u_scr[...], win_ref[:, 2 * D_MODEL + c:2 * D_MODEL + c + PROJ_CHUNK])
        zb_scr[:, c:c + PROJ_CHUNK] = z
        return _ordering_zero(z)

    def project_q(c):
        z = _dot(u_scr[main, :], win_ref[:, 5 * D_MODEL + c:5 * D_MODEL + c + PROJ_CHUNK])
        zq_scr[:, c:c + PROJ_CHUNK] = z
        return _ordering_zero(z)

    fillers = [functools.partial(project_b, c) for c in range(0, 3 * D_MODEL, PROJ_CHUNK)]
    fillers += [functools.partial(project_q, c) for c in range(0, 4 * D_MODEL, PROJ_CHUNK)]
    conv = _conv_a_blocks(sa_scr, wb_scr,
                          jnp.broadcast_to(cab_ref[...], (SUBLANES, D_MODEL)), ca_scr)
    _interleave(conv, (tm // SUBLANES) * (D_MODEL // LANES), fillers)

    a = ca_scr[...]
    mu = jnp.mean(a, axis=-1, keepdims=True)
    ac = a - mu
    var = jnp.mean(ac * ac, axis=-1, keepdims=True)
    a = jax.nn.silu(ac * lax.rsqrt(var + EPS) * lng_ref[...] + lnb_ref[...])
    y_a = _dot(a.astype(BF16), wao_ref[:, :D_MODEL])
    merged = jax.nn.sigmoid(zq_scr[:, col(1)]) * y_a

    sb_scr[...] = jnp.where(inside, zb_scr[:, col(2)] * zb_scr[:, col(0)], 0.0)
    b = cbw_ref[0:1, :] * sb_scr[pl.ds(HALO - 1, tm), :]
    for t in range(1, CONV_B_WIDTH):
        b = b + cbw_ref[t:t + 1, :] * sb_scr[pl.ds(HALO - 1 + t, tm), :]
    y_b = _dot((zb_scr[main, col(1)] * b).astype(BF16), wbo_ref[:, :D_MODEL])
    merged = merged + jax.nn.sigmoid(zq_scr[:, col(2)]) * y_b

    heads = []
    for h in range(N_HEADS):
        hs = slice(h * HEAD_DIM, (h + 1) * HEAD_DIM)
        s = lax.dot_general(zq_scr[:, hs].astype(BF16), k_ref[:, hs],
                            (((1,), (1,)), ((), ())), preferred_element_type=F32)
        p = jnp.exp(s - jnp.max(s, axis=-1, keepdims=True))
        denom = jnp.sum(p, axis=-1, keepdims=True)
        heads.append(_dot(p.astype(BF16), v_ref[:, hs]) / denom)
    o = jnp.concatenate(heads, axis=-1).astype(BF16)
    y_c = _dot(o, wxo_ref[:, :D_MODEL])
    merged = merged + jax.nn.sigmoid(zq_scr[:, col(3)]) * y_c

    o_ref[...] = xm_ref[...] + _dot(merged.astype(BF16), wo_ref[:, :D_MODEL])


def _mixer_call(x, k, v, seq_len, g, win, caw, cab, lng, lnb, wao, cbw, wbo, wxo, wo):
    rows = x.shape[0]
    tm = MIX_TILE
    tiles_per_seq = seq_len // tm
    halo_per_tile = tm // HALO
    last_halo_block = rows // HALO - 1
    d_in = win.shape[1]
    ext = tm + 2 * HALO

    tile = pl.BlockSpec((tm, D_MODEL), lambda i: (i, 0))
    prev = pl.BlockSpec((HALO, D_MODEL),
                        lambda i: (jnp.maximum(i * halo_per_tile - 1, 0), 0))
    nxt = pl.BlockSpec((HALO, D_MODEL),
                       lambda i: (jnp.minimum((i + 1) * halo_per_tile, last_halo_block), 0))
    kv_blk = pl.BlockSpec((None, N_MEM, D_MODEL), lambda i: (i // tiles_per_seq, 0, 0))
    sq = _resident(wao.shape)
    vec = _resident((1, D_MODEL))
    return pl.pallas_call(
        functools.partial(_mixer_kernel, tiles_per_seq=tiles_per_seq),
        grid=(rows // tm,),
        in_specs=[tile, prev, nxt, kv_blk, kv_blk, vec, _resident((D_MODEL, d_in)),
                  _resident((CONV_A_WIDTH, D_MODEL)), vec, vec, vec, sq,
                  _resident((CONV_B_WIDTH, D_MODEL)), sq, sq, sq],
        out_specs=tile,
        out_shape=jax.ShapeDtypeStruct((rows, D_MODEL), F32),
        scratch_shapes=[pltpu.VMEM((ext, D_MODEL), BF16),
                        pltpu.VMEM((ext, D_MODEL), F32),
                        pltpu.VMEM((ext, D_MODEL), F32),
                        pltpu.VMEM((SUBLANES * CONV_A_WIDTH, D_MODEL), F32),
                        pltpu.VMEM((tm, D_MODEL), F32),
                        pltpu.VMEM((ext, 3 * D_MODEL), F32),
                        pltpu.VMEM((tm, 4 * D_MODEL), F32)],
        compiler_params=pltpu.CompilerParams(
            dimension_semantics=("arbitrary",), vmem_limit_bytes=VMEM_LIMIT_BYTES),
        name="mixer",
    )(x, x, x, k, v, g, win, caw, cab, lng, lnb, wao, cbw, wbo, wxo, wo)


def _trunk(x, mem, p, final_norm):
    b, s, _ = x.shape
    depth = p["w_in"].shape[0]
    x2 = x.reshape(b * s, D_MODEL)
    row = lambda w: w.reshape(1, D_MODEL)
    for l in range(depth):
        k, v = _kv_call(mem, row(p["mem_norm"][l]), p["w_kv"][l])
        x2 = _ffn_call(x2, row(p["ffn1_norm"][l]), p["ffn1_wg"][l], p["ffn1_wu"][l],
                       p["ffn1_wd"][l])
        x2 = _mixer_call(x2, k, v, s, row(p["mix_norm"][l]), p["w_in"][l],
                         p["conv_a_w"][l], row(p["conv_a_b"][l]), row(p["ln_a_g"][l]),
                         row(p["ln_a_b"][l]), p["w_a_out"][l], p["conv_b_w"][l],
                         p["w_b_out"][l], p["w_x_out"][l], p["w_o"][l])
        x2 = _ffn_call(x2, row(p["ffn2_norm"][l]), p["ffn2_wg"][l], p["ffn2_wu"][l],
                       p["ffn2_wd"][l],
                       final_g=row(final_norm) if l == depth - 1 else None)
    return x2.reshape(b, s, D_MODEL)


def _mxu_weight(w):
    w = w.astype(BF16)
    if w.shape[-1] % (SUBLANES * LANES) == 0:
        w = jnp.pad(w, ((0, 0),) * (w.ndim - 1) + ((0, LANES),))
    return w


_MATMUL_WEIGHTS = ("ffn1_wg", "ffn1_wu", "ffn1_wd", "w_in", "w_a_out", "w_b_out",
                   "w_kv", "w_x_out", "w_o", "ffn2_wg", "ffn2_wu", "ffn2_wd")


def kernel(x_prompt, x_sample, mem_prompt, mem_sample, ffn1_norm, ffn1_wg, ffn1_wu, ffn1_wd, mix_norm, mem_norm, w_in, conv_a_w, conv_a_b, ln_a_g, ln_a_b, w_a_out, conv_b_w, w_b_out, w_kv, w_x_out, w_o, ffn2_norm, ffn2_wg, ffn2_wu, ffn2_wd, final_norm):
    p = dict(ffn1_norm=ffn1_norm, ffn1_wg=ffn1_wg, ffn1_wu=ffn1_wu, ffn1_wd=ffn1_wd,
             mix_norm=mix_norm, mem_norm=mem_norm, w_in=w_in, conv_a_w=conv_a_w,
             conv_a_b=conv_a_b, ln_a_g=ln_a_g, ln_a_b=ln_a_b, w_a_out=w_a_out,
             conv_b_w=conv_b_w, w_b_out=w_b_out, w_kv=w_kv, w_x_out=w_x_out, w_o=w_o,
             ffn2_norm=ffn2_norm, ffn2_wg=ffn2_wg, ffn2_wu=ffn2_wu, ffn2_wd=ffn2_wd)
    for name in _MATMUL_WEIGHTS:
        p[name] = _mxu_weight(p[name])
    y_prompt = _trunk(x_prompt, mem_prompt, p, final_norm)
    y_sample = _trunk(x_sample, mem_sample, p, final_norm)
    return (y_prompt, y_sample)
```

```python
import functools

import jax
import jax.numpy as jnp
from jax import lax
from jax.experimental import pallas as pl
from jax.experimental.pallas import tpu as pltpu

D_MODEL = 1024
D_FF = 2816
N_MEM = 256
N_HEADS = 4
HEAD_DIM = D_MODEL // N_HEADS
CONV_A_WIDTH = 31
CONV_B_WIDTH = 3
EPS = 1e-6

HALO = 16
SUBLANES = 8
LANES = 128
FFN_TILE = 1024
FFN_SUBTILE = 512
MIX_TILE = 256
PROJ_CHUNK = 512
FILLER_FIRST_STEP = 6
FILLER_PERIOD = 22
VMEM_LIMIT_BYTES = 56 * 1024 * 1024

F32 = jnp.float32
BF16 = jnp.bfloat16


def _rmsnorm(x, g):
    return x * lax.rsqrt(jnp.mean(x * x, axis=-1, keepdims=True) + EPS) * g


def _dot(a, b):
    return jnp.dot(a, b, preferred_element_type=F32)


def _layer_block(stacked, layer):
    shape = stacked.shape[1:]
    return pl.BlockSpec((None,) + shape, lambda i: (layer,) + (0,) * len(shape),
                        pipeline_mode=pl.Buffered(1))


def _kv_kernel(mem_ref, g_ref, wkv_ref, k_ref, v_ref):
    mem_n = _rmsnorm(mem_ref[...], g_ref[...]).astype(BF16)
    kv = _dot(mem_n, wkv_ref[:, :2 * D_MODEL])
    k_ref[...] = (kv[:, :D_MODEL] * (HEAD_DIM ** -0.5)).astype(BF16)
    v_ref[...] = kv[:, D_MODEL:].astype(BF16)


def _kv_call(mem, g, wkv, layer):
    b = mem.shape[0]
    out = jax.ShapeDtypeStruct((b, N_MEM, D_MODEL), BF16)
    blk = pl.BlockSpec((None, N_MEM, D_MODEL), lambda i: (i, 0, 0))
    return pl.pallas_call(
        _kv_kernel,
        grid=(b,),
        in_specs=[blk, _layer_block(g, layer), _layer_block(wkv, layer)],
        out_specs=[blk, blk],
        out_shape=[out, out],
        compiler_params=pltpu.CompilerParams(
            dimension_semantics=("arbitrary",), vmem_limit_bytes=VMEM_LIMIT_BYTES),
        name="kv_proj",
    )(mem, g, wkv)


def _ffn_kernel(x_ref, g_ref, wg_ref, wu_ref, wd_ref, *rest, final_norm):
    o_ref = rest[-1]
    for r in range(0, x_ref.shape[0], FFN_SUBTILE):
        rows = slice(r, r + FFN_SUBTILE)
        x = x_ref[rows, :]
        h = _rmsnorm(x, g_ref[...]).astype(BF16)
        gate = _dot(h, wg_ref[...])
        up = _dot(h, wu_ref[...])
        act = (jax.nn.silu(gate) * up).astype(BF16)
        y = x + 0.5 * _dot(act, wd_ref[:, :D_MODEL])
        if final_norm:
            y = _rmsnorm(y, rest[0][...])
        o_ref[rows, :] = y


def _ffn_call(x, layer, g, wg, wu, wd, final_g=None):
    rows = x.shape[0]
    tile = pl.BlockSpec((FFN_TILE, D_MODEL), lambda i: (i, 0))
    in_specs = [tile] + [_layer_block(w, layer) for w in (g, wg, wu, wd)]
    args = [x, g, wg, wu, wd]
    if final_g is not None:
        in_specs.append(_layer_block(final_g, 0))
        args.append(final_g)
    return pl.pallas_call(
        functools.partial(_ffn_kernel, final_norm=final_g is not None),
        grid=(rows // FFN_TILE,),
        in_specs=in_specs,
        out_specs=tile,
        out_shape=jax.ShapeDtypeStruct((rows, D_MODEL), F32),
        compiler_params=pltpu.CompilerParams(
            dimension_semantics=("arbitrary",), vmem_limit_bytes=VMEM_LIMIT_BYTES),
        name="ffn_final" if final_g is not None else "ffn",
    )(*args)


def _conv_a_blocks(s_scr, wb_scr, bias, out_scr):
    tm = out_scr.shape[0]
    first = HALO - CONV_A_WIDTH // 2
    nq = pl.cdiv(first + CONV_A_WIDTH, SUBLANES)
    sub = lax.broadcasted_iota(jnp.int32, (SUBLANES, LANES), 0)

    for c in range(0, D_MODEL, LANES):
        lanes = slice(c, c + LANES)
        w = [wb_scr[SUBLANES * k:SUBLANES * (k + 1), lanes] for k in range(CONV_A_WIDTH)]
        bias_c = bias[:, lanes]

        def rotated_partials(j):
            s = [s_scr[j + SUBLANES * q:j + SUBLANES * (q + 1), lanes] for q in range(nq)]
            parts = []
            for r in range(SUBLANES):
                acc = None
                for q in range(nq):
                    k = SUBLANES * q + r - first
                    if 0 <= k < CONV_A_WIDTH:
                        term = w[k] * s[q]
                        acc = term if acc is None else acc + term
                parts.append(acc if r == 0 else pltpu.roll(acc, SUBLANES - r, axis=0))
            return parts

        cur = rotated_partials(0)
        for blk in range(tm // SUBLANES):
            nxt = rotated_partials(SUBLANES * (blk + 1))
            acc = cur[0] + bias_c
            for r in range(1, SUBLANES):
                acc = acc + jnp.where(sub < SUBLANES - r, cur[r], nxt[r])
            after = yield
            if after is not None:
                acc = acc + after
            out_scr[SUBLANES * blk:SUBLANES * (blk + 1), lanes] = acc
            cur = nxt


def _ordering_zero(x):
    bits = x[-SUBLANES:, -LANES:].astype(jnp.int32)
    return lax.shift_right_logical(lax.shift_right_logical(bits, 16), 16).astype(F32)


def _interleave(steps, n_steps, fillers):
    zeros = [filler() for filler in fillers]
    next(steps)
    for j in range(n_steps):
        k = (j - FILLER_FIRST_STEP) // FILLER_PERIOD
        after = zeros[k] if 0 <= k < len(zeros) else None
        try:
            steps.send(after)
        except StopIteration:
            assert j == n_steps - 1


def _mixer_kernel(xm_ref, xp_ref, xn_ref, k_ref, v_ref, g_ref, win_ref,
                  caw_ref, cab_ref, lng_ref, lnb_ref, wao_ref, cbw_ref, wbo_ref,
                  wxo_ref, wo_ref, o_ref, u_scr, sa_scr, sb_scr, wb_scr, ca_scr,
                  zb_scr, zq_scr, *, tiles_per_seq):
    tm = xm_ref.shape[0]
    ext = tm + 2 * HALO
    main = slice(HALO, HALO + tm)
    col = lambda j: slice(j * D_MODEL, (j + 1) * D_MODEL)

    i = pl.program_id(0)
    first = (i % tiles_per_seq) == 0
    last = (i % tiles_per_seq) == tiles_per_seq - 1

    g = g_ref[...]
    u_scr[0:HALO, :] = _rmsnorm(xp_ref[...], g).astype(BF16)
    u_scr[main, :] = _rmsnorm(xm_ref[...], g).astype(BF16)
    u_scr[HALO + tm:ext, :] = _rmsnorm(xn_ref[...], g).astype(BF16)

    row = lax.broadcasted_iota(jnp.int32, (ext, 1), 0)
    inside = jnp.logical_and(jnp.logical_or(row >= HALO, jnp.logical_not(first)),
                             jnp.logical_or(row < HALO + tm, jnp.logical_not(last)))

    za = _dot(u_scr[...], win_ref[:, 0:2 * D_MODEL])
    a_in = za[:, col(0)] * jax.nn.sigmoid(za[:, col(1)])
    sa_scr[...] = jnp.where(inside, a_in, 0.0)
    for t in range(CONV_A_WIDTH):
        wb_scr[SUBLANES * t:SUBLANES * (t + 1), :] = jnp.broadcast_to(
            caw_ref[t:t + 1, :], (SUBLANES, D_MODEL))

    def project_b(c):
        z = _dot(u_scr[...], win_ref[:, 2 * D_MODEL + c:2 * D_MODEL + c + PROJ_CHUNK])
        zb_scr[:, c:c + PROJ_CHUNK] = z
        return _ordering_zero(z)

    def project_q(c):
        z = _dot(u_scr[main, :], win_ref[:, 5 * D_MODEL + c:5 * D_MODEL + c + PROJ_CHUNK])
        zq_scr[:, c:c + PROJ_CHUNK] = z
        return _ordering_zero(z)

    fillers = [functools.partial(project_b, c) for c in range(0, 3 * D_MODEL, PROJ_CHUNK)]
    fillers += [functools.partial(project_q, c) for c in range(0, 4 * D_MODEL, PROJ_CHUNK)]
    conv = _conv_a_blocks(sa_scr, wb_scr,
                          jnp.broadcast_to(cab_ref[...], (SUBLANES, D_MODEL)), ca_scr)
    _interleave(conv, (tm // SUBLANES) * (D_MODEL // LANES), fillers)

    a = ca_scr[...]
    mu = jnp.mean(a, axis=-1, keepdims=True)
    ac = a - mu
    var = jnp.mean(ac * ac, axis=-1, keepdims=True)
    a = jax.nn.silu(ac * lax.rsqrt(var + EPS) * lng_ref[...] + lnb_ref[...])
    y_a = _dot(a.astype(BF16), wao_ref[:, :D_MODEL])
    merged = jax.nn.sigmoid(zq_scr[:, col(1)]) * y_a

    sb_scr[...] = jnp.where(inside, zb_scr[:, col(2)] * zb_scr[:, col(0)], 0.0)
    b = cbw_ref[0:1, :] * sb_scr[pl.ds(HALO - 1, tm), :]
    for t in range(1, CONV_B_WIDTH):
        b = b + cbw_ref[t:t + 1, :] * sb_scr[pl.ds(HALO - 1 + t, tm), :]
    y_b = _dot((zb_scr[main, col(1)] * b).astype(BF16), wbo_ref[:, :D_MODEL])
    merged = merged + jax.nn.sigmoid(zq_scr[:, col(2)]) * y_b

    heads = []
    for h in range(N_HEADS):
        hs = slice(h * HEAD_DIM, (h + 1) * HEAD_DIM)
        s = lax.dot_general(zq_scr[:, hs].astype(BF16), k_ref[:, hs],
                            (((1,), (1,)), ((), ())), preferred_element_type=F32)
        p = jnp.exp(s - jnp.max(s, axis=-1, keepdims=True))
        denom = jnp.sum(p, axis=-1, keepdims=True)
        heads.append(_dot(p.astype(BF16), v_ref[:, hs]) / denom)
    o = jnp.concatenate(heads, axis=-1).astype(BF16)
    y_c = _dot(o, wxo_ref[:, :D_MODEL])
    merged = merged + jax.nn.sigmoid(zq_scr[:, col(3)]) * y_c

    o_ref[...] = xm_ref[...] + _dot(merged.astype(BF16), wo_ref[:, :D_MODEL])


def _mixer_call(x, k, v, seq_len, layer, *params):
    rows = x.shape[0]
    tm = MIX_TILE
    tiles_per_seq = seq_len // tm
    halo_per_tile = tm // HALO
    last_halo_block = rows // HALO - 1
    ext = tm + 2 * HALO

    tile = pl.BlockSpec((tm, D_MODEL), lambda i: (i, 0))
    prev = pl.BlockSpec((HALO, D_MODEL),
                        lambda i: (jnp.maximum(i * halo_per_tile - 1, 0), 0))
    nxt = pl.BlockSpec((HALO, D_MODEL),
                       lambda i: (jnp.minimum((i + 1) * halo_per_tile, last_halo_block), 0))
    kv_blk = pl.BlockSpec((None, N_MEM, D_MODEL), lambda i: (i // tiles_per_seq, 0, 0))
    return pl.pallas_call(
        functools.partial(_mixer_kernel, tiles_per_seq=tiles_per_seq),
        grid=(rows // tm,),
        in_specs=[tile, prev, nxt, kv_blk, kv_blk] + [_layer_block(w, layer) for w in params],
        out_specs=tile,
        out_shape=jax.ShapeDtypeStruct((rows, D_MODEL), F32),
        scratch_shapes=[pltpu.VMEM((ext, D_MODEL), BF16),
                        pltpu.VMEM((ext, D_MODEL), F32),
                        pltpu.VMEM((ext, D_MODEL), F32),
                        pltpu.VMEM((SUBLANES * CONV_A_WIDTH, D_MODEL), F32),
                        pltpu.VMEM((tm, D_MODEL), F32),
                        pltpu.VMEM((ext, 3 * D_MODEL), F32),
                        pltpu.VMEM((tm, 4 * D_MODEL), F32)],
        compiler_params=pltpu.CompilerParams(
            dimension_semantics=("arbitrary",), vmem_limit_bytes=VMEM_LIMIT_BYTES),
        name="mixer",
    )(x, x, x, k, v, *params)


_MIXER_PARAMS = ("mix_norm", "w_in", "conv_a_w", "conv_a_b", "ln_a_g", "ln_a_b", "w_a_out",
                 "conv_b_w", "w_b_out", "w_x_out", "w_o")


def _trunk(x, mem, p, final_norm):
    b, s, _ = x.shape
    x2 = x.reshape(b * s, D_MODEL)
    depth = p["w_in"].shape[0]
    for l in range(depth):
        k, v = _kv_call(mem, p["mem_norm"], p["w_kv"], l)
        x2 = _ffn_call(x2, l, p["ffn1_norm"], p["ffn1_wg"], p["ffn1_wu"], p["ffn1_wd"])
        x2 = _mixer_call(x2, k, v, s, l, *(p[name] for name in _MIXER_PARAMS))
        x2 = _ffn_call(x2, l, p["ffn2_norm"], p["ffn2_wg"], p["ffn2_wu"], p["ffn2_wd"],
                       final_g=final_norm if l == depth - 1 else None)
    return x2.reshape(b, s, D_MODEL)


def _mxu_weight(w):
    w = w.astype(BF16)
    if w.shape[-1] % (SUBLANES * LANES) == 0:
        w = jnp.pad(w, ((0, 0),) * (w.ndim - 1) + ((0, LANES),))
    return w


_MATMUL_WEIGHTS = ("ffn1_wg", "ffn1_wu", "ffn1_wd", "w_in", "w_a_out", "w_b_out",
                   "w_kv", "w_x_out", "w_o", "ffn2_wg", "ffn2_wu", "ffn2_wd")


def kernel(x_prompt, x_sample, mem_prompt, mem_sample, ffn1_norm, ffn1_wg, ffn1_wu, ffn1_wd, mix_norm, mem_norm, w_in, conv_a_w, conv_a_b, ln_a_g, ln_a_b, w_a_out, conv_b_w, w_b_out, w_kv, w_x_out, w_o, ffn2_norm, ffn2_wg, ffn2_wu, ffn2_wd, final_norm):
    p = dict(ffn1_norm=ffn1_norm, ffn1_wg=ffn1_wg, ffn1_wu=ffn1_wu, ffn1_wd=ffn1_wd,
             mix_norm=mix_norm, mem_norm=mem_norm, w_in=w_in, conv_a_w=conv_a_w,
             conv_a_b=conv_a_b, ln_a_g=ln_a_g, ln_a_b=ln_a_b, w_a_out=w_a_out,
             conv_b_w=conv_b_w, w_b_out=w_b_out, w_kv=w_kv, w_x_out=w_x_out, w_o=w_o,
             ffn2_norm=ffn2_norm, ffn2_wg=ffn2_wg, ffn2_wu=ffn2_wu, ffn2_wd=ffn2_wd)
    for name, w in p.items():
        p[name] = _mxu_weight(w) if name in _MATMUL_WEIGHTS else (
            w[:, None, :] if w.ndim == 2 else w)
    final_norm = final_norm.reshape(1, 1, D_MODEL)
    y_prompt = _trunk(x_prompt, mem_prompt, p, final_norm)
    y_sample = _trunk(x_sample, mem_sample, p, final_norm)
    return (y_prompt, y_sample)
```

```python
import functools

import jax
import jax.numpy as jnp
from jax import lax
from jax.experimental import pallas as pl
from jax.experimental.pallas import tpu as pltpu

D_MODEL = 1024
D_FF = 2816
N_MEM = 256
N_HEADS = 4
HEAD_DIM = D_MODEL // N_HEADS
CONV_A_WIDTH = 31
CONV_B_WIDTH = 3
EPS = 1e-6

HALO = 16
SUBLANES = 8
LANES = 128
FFN_TILE = 1024
FFN_SUBTILE = 512
MIX_TILE = 512
PROJ_CHUNK = 512
FILLER_FIRST_STEP = 12
FILLER_PERIOD = 44
VMEM_LIMIT_BYTES = 56 * 1024 * 1024

F32 = jnp.float32
BF16 = jnp.bfloat16


def _rmsnorm(x, g):
    return x * lax.rsqrt(jnp.mean(x * x, axis=-1, keepdims=True) + EPS) * g


def _dot(a, b):
    return jnp.dot(a, b, preferred_element_type=F32)


def _layer_block(stacked, layer):
    shape = stacked.shape[1:]
    return pl.BlockSpec((None,) + shape, lambda i: (layer,) + (0,) * len(shape),
                        pipeline_mode=pl.Buffered(1))


def _kv_kernel(mem_ref, g_ref, wkv_ref, k_ref, v_ref):
    mem_n = _rmsnorm(mem_ref[...], g_ref[...]).astype(BF16)
    kv = _dot(mem_n, wkv_ref[:, :2 * D_MODEL])
    k_ref[...] = (kv[:, :D_MODEL] * (HEAD_DIM ** -0.5)).astype(BF16)
    v_ref[...] = kv[:, D_MODEL:].astype(BF16)


def _kv_call(mem, g, wkv, layer):
    b = mem.shape[0]
    out = jax.ShapeDtypeStruct((b, N_MEM, D_MODEL), BF16)
    blk = pl.BlockSpec((None, N_MEM, D_MODEL), lambda i: (i, 0, 0))
    return pl.pallas_call(
        _kv_kernel,
        grid=(b,),
        in_specs=[blk, _layer_block(g, layer), _layer_block(wkv, layer)],
        out_specs=[blk, blk],
        out_shape=[out, out],
        compiler_params=pltpu.CompilerParams(
            dimension_semantics=("arbitrary",), vmem_limit_bytes=VMEM_LIMIT_BYTES),
        name="kv_proj",
    )(mem, g, wkv)


def _ffn_kernel(x_ref, g_ref, wg_ref, wu_ref, wd_ref, *rest, final_norm):
    o_ref = rest[-1]
    for r in range(0, x_ref.shape[0], FFN_SUBTILE):
        rows = slice(r, r + FFN_SUBTILE)
        x = x_ref[rows, :]
        h = _rmsnorm(x, g_ref[...]).astype(BF16)
        gate = _dot(h, wg_ref[...])
        up = _dot(h, wu_ref[...])
        act = (jax.nn.silu(gate) * up).astype(BF16)
        y = x + 0.5 * _dot(act, wd_ref[:, :D_MODEL])
        if final_norm:
            y = _rmsnorm(y, rest[0][...])
        o_ref[rows, :] = y


def _ffn_call(x, layer, g, wg, wu, wd, final_g=None):
    rows = x.shape[0]
    tile = pl.BlockSpec((FFN_TILE, D_MODEL), lambda i: (i, 0))
    in_specs = [tile] + [_layer_block(w, layer) for w in (g, wg, wu, wd)]
    args = [x, g, wg, wu, wd]
    if final_g is not None:
        in_specs.append(_layer_block(final_g, 0))
        args.append(final_g)
    return pl.pallas_call(
        functools.partial(_ffn_kernel, final_norm=final_g is not None),
        grid=(rows // FFN_TILE,),
        in_specs=in_specs,
        out_specs=tile,
        out_shape=jax.ShapeDtypeStruct((rows, D_MODEL), F32),
        compiler_params=pltpu.CompilerParams(
            dimension_semantics=("arbitrary",), vmem_limit_bytes=VMEM_LIMIT_BYTES),
        name="ffn_final" if final_g is not None else "ffn",
    )(*args)


def _conv_a_blocks(s_scr, wb_scr, bias, out_scr):
    tm = out_scr.shape[0]
    first = HALO - CONV_A_WIDTH // 2
    nq = pl.cdiv(first + CONV_A_WIDTH, SUBLANES)
    sub = lax.broadcasted_iota(jnp.int32, (SUBLANES, LANES), 0)

    for c in range(0, D_MODEL, LANES):
        lanes = slice(c, c + LANES)
        w = [wb_scr[SUBLANES * k:SUBLANES * (k + 1), lanes] for k in range(CONV_A_WIDTH)]
        bias_c = bias[:, lanes]

        def rotated_partials(j):
            s = [s_scr[j + SUBLANES * q:j + SUBLANES * (q + 1), lanes] for q in range(nq)]
            parts = []
            for r in range(SUBLANES):
                acc = None
                for q in range(nq):
                    k = SUBLANES * q + r - first
                    if 0 <= k < CONV_A_WIDTH:
                        term = w[k] * s[q]
                        acc = term if acc is None else acc + term
                parts.append(acc if r == 0 else pltpu.roll(acc, SUBLANES - r, axis=0))
            return parts

        cur = rotated_partials(0)
        for blk in range(tm // SUBLANES):
            nxt = rotated_partials(SUBLANES * (blk + 1))
            acc = cur[0] + bias_c
            for r in range(1, SUBLANES):
                acc = acc + jnp.where(sub < SUBLANES - r, cur[r], nxt[r])
            after = yield
            if after is not None:
                acc = acc + after
            out_scr[SUBLANES * blk:SUBLANES * (blk + 1), lanes] = acc
            cur = nxt


def _ordering_zero(x):
    bits = x[-SUBLANES:, -LANES:].astype(jnp.int32)
    return lax.shift_right_logical(lax.shift_right_logical(bits, 16), 16).astype(F32)


def _interleave(steps, n_steps, fillers):
    zeros = [filler() for filler in fillers]
    next(steps)
    for j in range(n_steps):
        k = (j - FILLER_FIRST_STEP) // FILLER_PERIOD
        after = zeros[k] if 0 <= k < len(zeros) else None
        try:
            steps.send(after)
        except StopIteration:
            assert j == n_steps - 1


def _mixer_kernel(xm_ref, xp_ref, xn_ref, k_ref, v_ref, g_ref, win_ref,
                  caw_ref, cab_ref, lng_ref, lnb_ref, wao_ref, cbw_ref, wbo_ref,
                  wxo_ref, wo_ref, o_ref, u_scr, sa_scr, sb_scr, wb_scr, ca_scr,
                  zb_scr, zq_scr, *, tiles_per_seq):
    tm = xm_ref.shape[0]
    ext = tm + 2 * HALO
    main = slice(HALO, HALO + tm)
    col = lambda j: slice(j * D_MODEL, (j + 1) * D_MODEL)

    i = pl.program_id(0)
    first = (i % tiles_per_seq) == 0
    last = (i % tiles_per_seq) == tiles_per_seq - 1

    g = g_ref[...]
    u_scr[0:HALO, :] = _rmsnorm(xp_ref[...], g).astype(BF16)
    u_scr[main, :] = _rmsnorm(xm_ref[...], g).astype(BF16)
    u_scr[HALO + tm:ext, :] = _rmsnorm(xn_ref[...], g).astype(BF16)

    row = lax.broadcasted_iota(jnp.int32, (ext, 1), 0)
    inside = jnp.logical_and(jnp.logical_or(row >= HALO, jnp.logical_not(first)),
                             jnp.logical_or(row < HALO + tm, jnp.logical_not(last)))

    za = _dot(u_scr[...], win_ref[:, 0:2 * D_MODEL])
    a_in = za[:, col(0)] * jax.nn.sigmoid(za[:, col(1)])
    sa_scr[...] = jnp.where(inside, a_in, 0.0)
    for t in range(CONV_A_WIDTH):
        wb_scr[SUBLANES * t:SUBLANES * (t + 1), :] = jnp.broadcast_to(
            caw_ref[t:t + 1, :], (SUBLANES, D_MODEL))

    def project_b(c):
        z = _dot(u_scr[...], win_ref[:, 2 * D_MODEL + c:2 * D_MODEL + c + PROJ_CHUNK])
        zb_scr[:, c:c + PROJ_CHUNK] = z
        return _ordering_zero(z)

    def project_q(c):
        z = _dot(u_scr[main, :], win_ref[:, 5 * D_MODEL + c:5 * D_MODEL + c + PROJ_CHUNK])
        zq_scr[:, c:c + PROJ_CHUNK] = z
        return _ordering_zero(z)

    fillers = [functools.partial(project_b, c) for c in range(0, 3 * D_MODEL, PROJ_CHUNK)]
    fillers += [functools.partial(project_q, c) for c in range(0, 4 * D_MODEL, PROJ_CHUNK)]
    conv = _conv_a_blocks(sa_scr, wb_scr,
                          jnp.broadcast_to(cab_ref[...], (SUBLANES, D_MODEL)), ca_scr)
    _interleave(conv, (tm // SUBLANES) * (D_MODEL // LANES), fillers)

    a = ca_scr[...]
    mu = jnp.mean(a, axis=-1, keepdims=True)
    ac = a - mu
    var = jnp.mean(ac * ac, axis=-1, keepdims=True)
    a = jax.nn.silu(ac * lax.rsqrt(var + EPS) * lng_ref[...] + lnb_ref[...])
    y_a = _dot(a.astype(BF16), wao_ref[:, :D_MODEL])
    merged = jax.nn.sigmoid(zq_scr[:, col(1)]) * y_a

    sb_scr[...] = jnp.where(inside, zb_scr[:, col(2)] * zb_scr[:, col(0)], 0.0)
    b = cbw_ref[0:1, :] * sb_scr[pl.ds(HALO - 1, tm), :]
    for t in range(1, CONV_B_WIDTH):
        b = b + cbw_ref[t:t + 1, :] * sb_scr[pl.ds(HALO - 1 + t, tm), :]
    y_b = _dot((zb_scr[main, col(1)] * b).astype(BF16), wbo_ref[:, :D_MODEL])
    merged = merged + jax.nn.sigmoid(zq_scr[:, col(2)]) * y_b

    heads = []
    for h in range(N_HEADS):
        hs = slice(h * HEAD_DIM, (h + 1) * HEAD_DIM)
        s = lax.dot_general(zq_scr[:, hs].astype(BF16), k_ref[:, hs],
                            (((1,), (1,)), ((), ())), preferred_element_type=F32)
        p = jnp.exp(s - jnp.max(s, axis=-1, keepdims=True))
        denom = jnp.sum(p, axis=-1, keepdims=True)
        heads.append(_dot(p.astype(BF16), v_ref[:, hs]) / denom)
    o = jnp.concatenate(heads, axis=-1).astype(BF16)
    y_c = _dot(o, wxo_ref[:, :D_MODEL])
    merged = merged + jax.nn.sigmoid(zq_scr[:, col(3)]) * y_c

    o_ref[...] = xm_ref[...] + _dot(merged.astype(BF16), wo_ref[:, :D_MODEL])


def _mixer_call(x, k, v, seq_len, layer, *params):
    rows = x.shape[0]
    tm = MIX_TILE
    tiles_per_seq = seq_len // tm
    halo_per_tile = tm // HALO
    last_halo_block = rows // HALO - 1
    ext = tm + 2 * HALO

    tile = pl.BlockSpec((tm, D_MODEL), lambda i: (i, 0))
    prev = pl.BlockSpec((HALO, D_MODEL),
                        lambda i: (jnp.maximum(i * halo_per_tile - 1, 0), 0))
    nxt = pl.BlockSpec((HALO, D_MODEL),
                       lambda i: (jnp.minimum((i + 1) * halo_per_tile, last_halo_block), 0))
    kv_blk = pl.BlockSpec((None, N_MEM, D_MODEL), lambda i: (i // tiles_per_seq, 0, 0))
    return pl.pallas_call(
        functools.partial(_mixer_kernel, tiles_per_seq=tiles_per_seq),
        grid=(rows // tm,),
        in_specs=[tile, prev, nxt, kv_blk, kv_blk] + [_layer_block(w, layer) for w in params],
        out_specs=tile,
        out_shape=jax.ShapeDtypeStruct((rows, D_MODEL), F32),
        scratch_shapes=[pltpu.VMEM((ext, D_MODEL), BF16),
                        pltpu.VMEM((ext, D_MODEL), F32),
                        pltpu.VMEM((ext, D_MODEL), F32),
                        pltpu.VMEM((SUBLANES * CONV_A_WIDTH, D_MODEL), F32),
                        pltpu.VMEM((tm, D_MODEL), F32),
                        pltpu.VMEM((ext, 3 * D_MODEL), F32),
                        pltpu.VMEM((tm, 4 * D_MODEL), F32)],
        compiler_params=pltpu.CompilerParams(
            dimension_semantics=("arbitrary",), vmem_limit_bytes=VMEM_LIMIT_BYTES),
        name="mixer",
    )(x, x, x, k, v, *params)


_MIXER_PARAMS = ("mix_norm", "w_in", "conv_a_w", "conv_a_b", "ln_a_g", "ln_a_b", "w_a_out",
                 "conv_b_w", "w_b_out", "w_x_out", "w_o")


def _trunk(x, mem, p, final_norm):
    b, s, _ = x.shape
    x2 = x.reshape(b * s, D_MODEL)
    depth = p["w_in"].shape[0]
    for l in range(depth):
        k, v = _kv_call(mem, p["mem_norm"], p["w_kv"], l)
        x2 = _ffn_call(x2, l, p["ffn1_norm"], p["ffn1_wg"], p["ffn1_wu"], p["ffn1_wd"])
        x2 = _mixer_call(x2, k, v, s, l, *(p[name] for name in _MIXER_PARAMS))
        x2 = _ffn_call(x2, l, p["ffn2_norm"], p["ffn2_wg"], p["ffn2_wu"], p["ffn2_wd"],
                       final_g=final_norm if l == depth - 1 else None)
    return x2.reshape(b, s, D_MODEL)


def _mxu_weight(w):
    if w.shape[-1] % (SUBLANES * LANES) == 0:
        w = jnp.pad(w, ((0, 0),) * (w.ndim - 1) + ((0, LANES),))
    return w.astype(BF16)


_MATMUL_WEIGHTS = ("ffn1_wg", "ffn1_wu", "ffn1_wd", "w_in", "w_a_out", "w_b_out",
                   "w_kv", "w_x_out", "w_o", "ffn2_wg", "ffn2_wu", "ffn2_wd")


def kernel(x_prompt, x_sample, mem_prompt, mem_sample, ffn1_norm, ffn1_wg, ffn1_wu, ffn1_wd, mix_norm, mem_norm, w_in, conv_a_w, conv_a_b, ln_a_g, ln_a_b, w_a_out, conv_b_w, w_b_out, w_kv, w_x_out, w_o, ffn2_norm, ffn2_wg, ffn2_wu, ffn2_wd, final_norm):
    p = dict(ffn1_norm=ffn1_norm, ffn1_wg=ffn1_wg, ffn1_wu=ffn1_wu, ffn1_wd=ffn1_wd,
             mix_norm=mix_norm, mem_norm=mem_norm, w_in=w_in, conv_a_w=conv_a_w,
             conv_a_b=conv_a_b, ln_a_g=ln_a_g, ln_a_b=ln_a_b, w_a_out=w_a_out,
             conv_b_w=conv_b_w, w_b_out=w_b_out, w_kv=w_kv, w_x_out=w_x_out, w_o=w_o,
             ffn2_norm=ffn2_norm, ffn2_wg=ffn2_wg, ffn2_wu=ffn2_wu, ffn2_wd=ffn2_wd)
    for name, w in p.items():
        p[name] = _mxu_weight(w) if name in _MATMUL_WEIGHTS else (
            w[:, None, :] if w.ndim == 2 else w)
    final_norm = final_norm.reshape(1, 1, D_MODEL)
    y_prompt = _trunk(x_prompt, mem_prompt, p, final_norm)
    y_sample = _trunk(x_sample, mem_sample, p, final_norm)
    return (y_prompt, y_sample)
```

```python
import functools

import jax
import jax.numpy as jnp
from jax import lax
from jax.experimental import pallas as pl
from jax.experimental.pallas import tpu as pltpu

D_MODEL = 1024
D_FF = 2816
N_MEM = 256
N_HEADS = 4
HEAD_DIM = D_MODEL // N_HEADS
CONV_A_WIDTH = 31
CONV_B_WIDTH = 3
EPS = 1e-6

HALO = 16
SUBLANES = 8
LANES = 128
FFN_TILE = 1024
FFN_SUBTILE = 512
MIX_TILE = 512
PROJ_CHUNK = 512
FILLER_FIRST_STEP = 12
FILLER_PERIOD = 44
VMEM_LIMIT_BYTES = 56 * 1024 * 1024

F32 = jnp.float32
BF16 = jnp.bfloat16


def _rmsnorm(x, g):
    return x * lax.rsqrt(jnp.mean(x * x, axis=-1, keepdims=True) + EPS) * g


def _dot(a, b):
    return jnp.dot(a, b, preferred_element_type=F32)


def _layer_block(stacked, layer):
    shape = stacked.shape[1:]
    return pl.BlockSpec((None,) + shape, lambda i: (layer,) + (0,) * len(shape),
                        pipeline_mode=pl.Buffered(1))


def _kv_kernel(mem_ref, g_ref, wkv_ref, k_ref, v_ref):
    mem_n = _rmsnorm(mem_ref[...], g_ref[...]).astype(BF16)
    kv = _dot(mem_n, wkv_ref[:, :2 * D_MODEL])
    k_ref[...] = (kv[:, :D_MODEL] * (HEAD_DIM ** -0.5)).astype(BF16)
    v_ref[...] = kv[:, D_MODEL:].astype(BF16)


def _kv_call(mem, g, wkv, layer):
    b = mem.shape[0]
    out = jax.ShapeDtypeStruct((b, N_MEM, D_MODEL), BF16)
    blk = pl.BlockSpec((None, N_MEM, D_MODEL), lambda i: (i, 0, 0))
    return pl.pallas_call(
        _kv_kernel,
        grid=(b,),
        in_specs=[blk, _layer_block(g, layer), _layer_block(wkv, layer)],
        out_specs=[blk, blk],
        out_shape=[out, out],
        compiler_params=pltpu.CompilerParams(
            dimension_semantics=("arbitrary",), vmem_limit_bytes=VMEM_LIMIT_BYTES),
        name="kv_proj",
    )(mem, g, wkv)


def _ffn_kernel(x_ref, g_ref, wg_ref, wu_ref, wd_ref, *rest, post):
    post_g_ref, outs = (rest[0], rest[1:]) if post else (None, rest)
    for r in range(0, x_ref.shape[0], FFN_SUBTILE):
        rows = slice(r, r + FFN_SUBTILE)
        x = x_ref[rows, :]
        h = _rmsnorm(x, g_ref[...]).astype(BF16)
        gate = _dot(h, wg_ref[...])
        up = _dot(h, wu_ref[...])
        act = (jax.nn.silu(gate) * up).astype(BF16)
        y = x + 0.5 * _dot(act, wd_ref[:, :D_MODEL])
        if post == "final":
            outs[0][rows, :] = _rmsnorm(y, post_g_ref[...])
        else:
            outs[0][rows, :] = y
            if post == "emit":
                outs[1][rows, :] = _rmsnorm(y, post_g_ref[...]).astype(BF16)


def _ffn_call(x, layer, g, wg, wu, wd, post=None, post_g=None, post_layer=0):
    rows = x.shape[0]
    tile = pl.BlockSpec((FFN_TILE, D_MODEL), lambda i: (i, 0))
    in_specs = [tile] + [_layer_block(w, layer) for w in (g, wg, wu, wd)]
    args = [x, g, wg, wu, wd]
    if post:
        in_specs.append(_layer_block(post_g, post_layer))
        args.append(post_g)
    y_shape = jax.ShapeDtypeStruct((rows, D_MODEL), F32)
    emit = post == "emit"
    return pl.pallas_call(
        functools.partial(_ffn_kernel, post=post),
        grid=(rows // FFN_TILE,),
        in_specs=in_specs,
        out_specs=[tile, tile] if emit else tile,
        out_shape=[y_shape, jax.ShapeDtypeStruct((rows, D_MODEL), BF16)] if emit else y_shape,
        compiler_params=pltpu.CompilerParams(
            dimension_semantics=("arbitrary",), vmem_limit_bytes=VMEM_LIMIT_BYTES),
        name="ffn_" + post if post else "ffn",
    )(*args)


def _conv_a_blocks(s_scr, wb_scr, bias, out_scr):
    tm = out_scr.shape[0]
    first = HALO - CONV_A_WIDTH // 2
    nq = pl.cdiv(first + CONV_A_WIDTH, SUBLANES)
    sub = lax.broadcasted_iota(jnp.int32, (SUBLANES, LANES), 0)

    for c in range(0, D_MODEL, LANES):
        lanes = slice(c, c + LANES)
        w = [wb_scr[SUBLANES * k:SUBLANES * (k + 1), lanes] for k in range(CONV_A_WIDTH)]
        bias_c = bias[:, lanes]

        def rotated_partials(j):
            s = [s_scr[j + SUBLANES * q:j + SUBLANES * (q + 1), lanes] for q in range(nq)]
            parts = []
            for r in range(SUBLANES):
                acc = None
                for q in range(nq):
                    k = SUBLANES * q + r - first
                    if 0 <= k < CONV_A_WIDTH:
                        term = w[k] * s[q]
                        acc = term if acc is None else acc + term
                parts.append(acc if r == 0 else pltpu.roll(acc, SUBLANES - r, axis=0))
            return parts

        cur = rotated_partials(0)
        for blk in range(tm // SUBLANES):
            nxt = rotated_partials(SUBLANES * (blk + 1))
            acc = cur[0] + bias_c
            for r in range(1, SUBLANES):
                acc = acc + jnp.where(sub < SUBLANES - r, cur[r], nxt[r])
            after = yield
            if after is not None:
                acc = acc + after
            out_scr[SUBLANES * blk:SUBLANES * (blk + 1), lanes] = acc
            cur = nxt


def _ordering_zero(x):
    bits = x[-SUBLANES:, -LANES:].astype(jnp.int32)
    return lax.shift_right_logical(lax.shift_right_logical(bits, 16), 16).astype(F32)


def _store_conv_input(scr, x, first, last):
    body = scr.shape[0] - HALO
    scr[0:HALO, :] = jnp.where(first, 0.0, x[0:HALO, :])
    scr[HALO:body, :] = x[HALO:body, :]
    scr[body:, :] = jnp.where(last, 0.0, x[body:, :])


def _interleave(steps, n_steps, fillers):
    zeros = [filler() for filler in fillers]
    next(steps)
    for j in range(n_steps):
        k = (j - FILLER_FIRST_STEP) // FILLER_PERIOD
        after = zeros[k] if 0 <= k < len(zeros) else None
        try:
            steps.send(after)
        except StopIteration:
            assert j == n_steps - 1


def _mixer_kernel(xm_ref, um_ref, up_ref, un_ref, k_ref, v_ref, win_ref,
                  caw_ref, cab_ref, lng_ref, lnb_ref, wao_ref, cbw_ref, wbo_ref,
                  wxo_ref, wo_ref, o_ref, u_scr, sa_scr, sb_scr, wb_scr, ca_scr,
                  zb_scr, zq_scr, *, tiles_per_seq):
    tm = xm_ref.shape[0]
    ext = tm + 2 * HALO
    main = slice(HALO, HALO + tm)
    col = lambda j: slice(j * D_MODEL, (j + 1) * D_MODEL)

    i = pl.program_id(0)
    first = (i % tiles_per_seq) == 0
    last = (i % tiles_per_seq) == tiles_per_seq - 1

    u_scr[0:HALO, :] = up_ref[...]
    u_scr[main, :] = um_ref[...]
    u_scr[HALO + tm:ext, :] = un_ref[...]

    za = _dot(u_scr[...], win_ref[:, 0:2 * D_MODEL])
    _store_conv_input(sa_scr, za[:, col(0)] * jax.nn.sigmoid(za[:, col(1)]), first, last)
    for t in range(CONV_A_WIDTH):
        wb_scr[SUBLANES * t:SUBLANES * (t + 1), :] = jnp.broadcast_to(
            caw_ref[t:t + 1, :], (SUBLANES, D_MODEL))

    def project_b(c):
        z = _dot(u_scr[...], win_ref[:, 2 * D_MODEL + c:2 * D_MODEL + c + PROJ_CHUNK])
        zb_scr[:, c:c + PROJ_CHUNK] = z
        return _ordering_zero(z)

    def project_q(c):
        z = _dot(u_scr[main, :], win_ref[:, 5 * D_MODEL + c:5 * D_MODEL + c + PROJ_CHUNK])
        zq_scr[:, c:c + PROJ_CHUNK] = z
        return _ordering_zero(z)

    fillers = [functools.partial(project_b, c) for c in range(0, 3 * D_MODEL, PROJ_CHUNK)]
    fillers += [functools.partial(project_q, c) for c in range(0, 4 * D_MODEL, PROJ_CHUNK)]
    conv = _conv_a_blocks(sa_scr, wb_scr,
                          jnp.broadcast_to(cab_ref[...], (SUBLANES, D_MODEL)), ca_scr)
    _interleave(conv, (tm // SUBLANES) * (D_MODEL // LANES), fillers)

    a = ca_scr[...]
    mu = jnp.mean(a, axis=-1, keepdims=True)
    ac = a - mu
    var = jnp.mean(ac * ac, axis=-1, keepdims=True)
    a = jax.nn.silu(ac * lax.rsqrt(var + EPS) * lng_ref[...] + lnb_ref[...])
    y_a = _dot(a.astype(BF16), wao_ref[:, :D_MODEL])
    merged = jax.nn.sigmoid(zq_scr[:, col(1)]) * y_a

    _store_conv_input(sb_scr, zb_scr[:, col(2)] * zb_scr[:, col(0)], first, last)
    b = cbw_ref[0:1, :] * sb_scr[pl.ds(HALO - 1, tm), :]
    for t in range(1, CONV_B_WIDTH):
        b = b + cbw_ref[t:t + 1, :] * sb_scr[pl.ds(HALO - 1 + t, tm), :]
    y_b = _dot((zb_scr[main, col(1)] * b).astype(BF16), wbo_ref[:, :D_MODEL])
    merged = merged + jax.nn.sigmoid(zq_scr[:, col(2)]) * y_b

    heads = []
    for h in range(N_HEADS):
        hs = slice(h * HEAD_DIM, (h + 1) * HEAD_DIM)
        s = lax.dot_general(zq_scr[:, hs].astype(BF16), k_ref[:, hs],
                            (((1,), (1,)), ((), ())), preferred_element_type=F32)
        p = jnp.exp(s - jnp.max(s, axis=-1, keepdims=True))
        denom = jnp.sum(p, axis=-1, keepdims=True)
        heads.append(_dot(p.astype(BF16), v_ref[:, hs]) / denom)
    o = jnp.concatenate(heads, axis=-1).astype(BF16)
    y_c = _dot(o, wxo_ref[:, :D_MODEL])
    merged = merged + jax.nn.sigmoid(zq_scr[:, col(3)]) * y_c

    o_ref[...] = xm_ref[...] + _dot(merged.astype(BF16), wo_ref[:, :D_MODEL])


def _mixer_call(x, u, k, v, seq_len, layer, *params):
    rows = x.shape[0]
    tm = MIX_TILE
    tiles_per_seq = seq_len // tm
    halo_per_tile = tm // HALO
    last_halo_block = rows // HALO - 1
    ext = tm + 2 * HALO

    tile = pl.BlockSpec((tm, D_MODEL), lambda i: (i, 0))
    prev = pl.BlockSpec((HALO, D_MODEL),
                        lambda i: (jnp.maximum(i * halo_per_tile - 1, 0), 0))
    nxt = pl.BlockSpec((HALO, D_MODEL),
                       lambda i: (jnp.minimum((i + 1) * halo_per_tile, last_halo_block), 0))
    kv_blk = pl.BlockSpec((None, N_MEM, D_MODEL), lambda i: (i // tiles_per_seq, 0, 0))
    return pl.pallas_call(
        functools.partial(_mixer_kernel, tiles_per_seq=tiles_per_seq),
        grid=(rows // tm,),
        in_specs=[tile, tile, prev, nxt, kv_blk, kv_blk] + [_layer_block(w, layer) for w in params],
        out_specs=tile,
        out_shape=jax.ShapeDtypeStruct((rows, D_MODEL), F32),
        scratch_shapes=[pltpu.VMEM((ext, D_MODEL), BF16),
                        pltpu.VMEM((ext, D_MODEL), F32),
                        pltpu.VMEM((ext, D_MODEL), F32),
                        pltpu.VMEM((SUBLANES * CONV_A_WIDTH, D_MODEL), F32),
                        pltpu.VMEM((tm, D_MODEL), F32),
                        pltpu.VMEM((ext, 3 * D_MODEL), F32),
                        pltpu.VMEM((tm, 4 * D_MODEL), F32)],
        compiler_params=pltpu.CompilerParams(
            dimension_semantics=("arbitrary",), vmem_limit_bytes=VMEM_LIMIT_BYTES),
        name="mixer",
    )(x, u, u, u, k, v, *params)


_MIXER_PARAMS = ("w_in", "conv_a_w", "conv_a_b", "ln_a_g", "ln_a_b", "w_a_out",
                 "conv_b_w", "w_b_out", "w_x_out", "w_o")


def _trunk(x, mem, p, final_norm):
    b, s, _ = x.shape
    x2 = x.reshape(b * s, D_MODEL)
    depth = p["w_in"].shape[0]
    for l in range(depth):
        k, v = _kv_call(mem, p["mem_norm"], p["w_kv"], l)
        x2, u = _ffn_call(x2, l, p["ffn1_norm"], p["ffn1_wg"], p["ffn1_wu"], p["ffn1_wd"],
                          post="emit", post_g=p["mix_norm"], post_layer=l)
        x2 = _mixer_call(x2, u, k, v, s, l, *(p[name] for name in _MIXER_PARAMS))
        last = l == depth - 1
        x2 = _ffn_call(x2, l, p["ffn2_norm"], p["ffn2_wg"], p["ffn2_wu"], p["ffn2_wd"],
                       post="final" if last else None, post_g=final_norm if last else None)
    return x2.reshape(b, s, D_MODEL)


def _mxu_weight(w):
    if w.shape[-1] % (SUBLANES * LANES) == 0:
        w = jnp.pad(w, ((0, 0),) * (w.ndim - 1) + ((0, LANES),))
    return w.astype(BF16)


_MATMUL_WEIGHTS = ("ffn1_wg", "ffn1_wu", "ffn1_wd", "w_in", "w_a_out", "w_b_out",
                   "w_kv", "w_x_out", "w_o", "ffn2_wg", "ffn2_wu", "ffn2_wd")


def kernel(x_prompt, x_sample, mem_prompt, mem_sample, ffn1_norm, ffn1_wg, ffn1_wu, ffn1_wd, mix_norm, mem_norm, w_in, conv_a_w, conv_a_b, ln_a_g, ln_a_b, w_a_out, conv_b_w, w_b_out, w_kv, w_x_out, w_o, ffn2_norm, ffn2_wg, ffn2_wu, ffn2_wd, final_norm):
    p = dict(ffn1_norm=ffn1_norm, ffn1_wg=ffn1_wg, ffn1_wu=ffn1_wu, ffn1_wd=ffn1_wd,
             mix_norm=mix_norm, mem_norm=mem_norm, w_in=w_in, conv_a_w=conv_a_w,
             conv_a_b=conv_a_b, ln_a_g=ln_a_g, ln_a_b=ln_a_b, w_a_out=w_a_out,
             conv_b_w=conv_b_w, w_b_out=w_b_out, w_kv=w_kv, w_x_out=w_x_out, w_o=w_o,
             ffn2_norm=ffn2_norm, ffn2_wg=ffn2_wg, ffn2_wu=ffn2_wu, ffn2_wd=ffn2_wd)
    for name, w in p.items():
        p[name] = _mxu_weight(w) if name in _MATMUL_WEIGHTS else (
            w[:, None, :] if w.ndim == 2 else w)
    final_norm = final_norm.reshape(1, 1, D_MODEL)
    y_prompt = _trunk(x_prompt, mem_prompt, p, final_norm)
    y_sample = _trunk(x_sample, mem_sample, p, final_norm)
    return (y_prompt, y_sample)
```

```python
import functools

import jax
import jax.numpy as jnp
from jax import lax
from jax.experimental import pallas as pl
from jax.experimental.pallas import tpu as pltpu

D_MODEL = 1024
N_MEM = 256
N_HEADS = 4
HEAD_DIM = D_MODEL // N_HEADS
CONV_A_WIDTH = 31
CONV_B_WIDTH = 3
EPS = 1e-6

HALO = 16
SUBLANES = 8
LANES = 128
ROW_PAIR_LAYOUT_COLUMNS = 8 * LANES
FFN_TILE = 1024
FFN_SUBTILE = 512
MIX_TILE = 512
PROJ_CHUNK = 512
FILLER_FIRST_STEP = 12
FILLER_PERIOD = 44
VMEM_LIMIT_BYTES = 56 * 1024 * 1024

F32 = jnp.float32
BF16 = jnp.bfloat16


def _rmsnorm(x, g):
    return x * lax.rsqrt(jnp.mean(x * x, axis=-1, keepdims=True) + EPS) * g


def _dot(a, b):
    return jnp.dot(a, b, preferred_element_type=F32)


def _layer_block(stacked, layer):
    shape = stacked.shape[1:]
    return pl.BlockSpec((None,) + shape, lambda i: (layer,) + (0,) * len(shape),
                        pipeline_mode=pl.Buffered(1))


def _kv_kernel(mem_ref, g_ref, wkv_ref, k_ref, v_ref):
    mem_n = _rmsnorm(mem_ref[...], g_ref[...]).astype(BF16)
    kv = _dot(mem_n, wkv_ref[:, :2 * D_MODEL])
    k_ref[...] = (kv[:, :D_MODEL] * (HEAD_DIM ** -0.5)).astype(BF16)
    v_ref[...] = kv[:, D_MODEL:].astype(BF16)


def _kv_call(mem, g, wkv, layer):
    b = mem.shape[0]
    out = jax.ShapeDtypeStruct((b, N_MEM, D_MODEL), BF16)
    blk = pl.BlockSpec((None, N_MEM, D_MODEL), lambda i: (i, 0, 0))
    return pl.pallas_call(
        _kv_kernel,
        grid=(b,),
        in_specs=[blk, _layer_block(g, layer), _layer_block(wkv, layer)],
        out_specs=[blk, blk],
        out_shape=[out, out],
        compiler_params=pltpu.CompilerParams(
            dimension_semantics=("arbitrary",), vmem_limit_bytes=VMEM_LIMIT_BYTES),
        name="kv_proj",
    )(mem, g, wkv)


def _ffn_kernel(x_ref, g_ref, wg_ref, wu_ref, wd_ref, *rest, post):
    post_g_ref, outs = (rest[0], rest[1:]) if post else (None, rest)
    for r in range(0, x_ref.shape[0], FFN_SUBTILE):
        rows = slice(r, r + FFN_SUBTILE)
        x = x_ref[rows, :]
        h = _rmsnorm(x, g_ref[...]).astype(BF16)
        gate = _dot(h, wg_ref[...])
        up = _dot(h, wu_ref[...])
        act = (jax.nn.silu(gate) * up).astype(BF16)
        y = x + 0.5 * _dot(act, wd_ref[:, :D_MODEL])
        if post == "final":
            outs[0][rows, :] = _rmsnorm(y, post_g_ref[...])
        else:
            outs[0][rows, :] = y
            if post == "emit":
                outs[1][rows, :] = _rmsnorm(y, post_g_ref[...]).astype(BF16)


def _ffn_call(x, layer, g, wg, wu, wd, post=None, post_g=None, post_layer=0):
    rows = x.shape[0]
    assert x.shape[1] == D_MODEL and rows % FFN_TILE == 0 and FFN_TILE % FFN_SUBTILE == 0
    tile = pl.BlockSpec((FFN_TILE, D_MODEL), lambda i: (i, 0))
    in_specs = [tile] + [_layer_block(w, layer) for w in (g, wg, wu, wd)]
    args = [x, g, wg, wu, wd]
    if post:
        in_specs.append(_layer_block(post_g, post_layer))
        args.append(post_g)
    y_shape = jax.ShapeDtypeStruct((rows, D_MODEL), F32)
    emit = post == "emit"
    return pl.pallas_call(
        functools.partial(_ffn_kernel, post=post),
        grid=(rows // FFN_TILE,),
        in_specs=in_specs,
        out_specs=[tile, tile] if emit else tile,
        out_shape=[y_shape, jax.ShapeDtypeStruct((rows, D_MODEL), BF16)] if emit else y_shape,
        compiler_params=pltpu.CompilerParams(
            dimension_semantics=("arbitrary",), vmem_limit_bytes=VMEM_LIMIT_BYTES),
        name="ffn_" + post if post else "ffn",
    )(*args)


def _conv_a_blocks(s_scr, wb_scr, bias, out_scr):
    tm = out_scr.shape[0]
    first = HALO - CONV_A_WIDTH // 2
    nq = pl.cdiv(first + CONV_A_WIDTH, SUBLANES)
    sub = lax.broadcasted_iota(jnp.int32, (SUBLANES, LANES), 0)

    for c in range(0, D_MODEL, LANES):
        lanes = slice(c, c + LANES)
        w = [wb_scr[SUBLANES * k:SUBLANES * (k + 1), lanes] for k in range(CONV_A_WIDTH)]
        bias_c = bias[:, lanes]

        def rotated_partials(j):
            s = [s_scr[j + SUBLANES * q:j + SUBLANES * (q + 1), lanes] for q in range(nq)]
            parts = []
            for r in range(SUBLANES):
                acc = None
                for q in range(nq):
                    k = SUBLANES * q + r - first
                    if 0 <= k < CONV_A_WIDTH:
                        term = w[k] * s[q]
                        acc = term if acc is None else acc + term
                parts.append(acc if r == 0 else pltpu.roll(acc, SUBLANES - r, axis=0))
            return parts

        cur = rotated_partials(0)
        for blk in range(tm // SUBLANES):
            nxt = rotated_partials(SUBLANES * (blk + 1))
            acc = cur[0] + bias_c
            for r in range(1, SUBLANES):
                acc = acc + jnp.where(sub < SUBLANES - r, cur[r], nxt[r])
            after = yield
            if after is not None:
                acc = acc + after
            out_scr[SUBLANES * blk:SUBLANES * (blk + 1), lanes] = acc
            cur = nxt


def _ordering_zero(x):
    bits = x[-SUBLANES:, -LANES:].astype(jnp.int32)
    return lax.shift_right_logical(lax.shift_right_logical(bits, 16), 16).astype(F32)


def _store_conv_input(scr, x, first, last):
    body = scr.shape[0] - HALO
    scr[0:HALO, :] = jnp.where(first, 0.0, x[0:HALO, :])
    scr[HALO:body, :] = x[HALO:body, :]
    scr[body:, :] = jnp.where(last, 0.0, x[body:, :])


def _interleave(steps, n_steps, fillers):
    zeros = [filler() for filler in fillers]
    next(steps)
    for j in range(n_steps):
        k = (j - FILLER_FIRST_STEP) // FILLER_PERIOD
        after = zeros[k] if 0 <= k < len(zeros) else None
        try:
            steps.send(after)
        except StopIteration:
            assert j == n_steps - 1


def _mixer_kernel(xm_ref, um_ref, up_ref, un_ref, k_ref, v_ref, win_ref,
                  caw_ref, cab_ref, lng_ref, lnb_ref, wao_ref, cbw_ref, wbo_ref,
                  wxo_ref, wo_ref, o_ref, u_scr, sa_scr, sb_scr, wb_scr, ca_scr,
                  zb_scr, zq_scr, *, tiles_per_seq):
    tm = xm_ref.shape[0]
    ext = tm + 2 * HALO
    main = slice(HALO, HALO + tm)
    col = lambda j: slice(j * D_MODEL, (j + 1) * D_MODEL)

    i = pl.program_id(0)
    first = (i % tiles_per_seq) == 0
    last = (i % tiles_per_seq) == tiles_per_seq - 1

    u_scr[0:HALO, :] = up_ref[...]
    u_scr[main, :] = um_ref[...]
    u_scr[HALO + tm:ext, :] = un_ref[...]

    za = _dot(u_scr[...], win_ref[:, 0:2 * D_MODEL])
    _store_conv_input(sa_scr, za[:, col(0)] * jax.nn.sigmoid(za[:, col(1)]), first, last)
    for t in range(CONV_A_WIDTH):
        wb_scr[SUBLANES * t:SUBLANES * (t + 1), :] = jnp.broadcast_to(
            caw_ref[t:t + 1, :], (SUBLANES, D_MODEL))

    def project_b(c):
        z = _dot(u_scr[...], win_ref[:, 2 * D_MODEL + c:2 * D_MODEL + c + PROJ_CHUNK])
        zb_scr[:, c:c + PROJ_CHUNK] = z
        return _ordering_zero(z)

    def project_q(c):
        z = _dot(u_scr[main, :], win_ref[:, 5 * D_MODEL + c:5 * D_MODEL + c + PROJ_CHUNK])
        zq_scr[:, c:c + PROJ_CHUNK] = z
        return _ordering_zero(z)

    fillers = [functools.partial(project_b, c) for c in range(0, 3 * D_MODEL, PROJ_CHUNK)]
    fillers += [functools.partial(project_q, c) for c in range(0, 4 * D_MODEL, PROJ_CHUNK)]
    conv = _conv_a_blocks(sa_scr, wb_scr,
                          jnp.broadcast_to(cab_ref[...], (SUBLANES, D_MODEL)), ca_scr)
    _interleave(conv, (tm // SUBLANES) * (D_MODEL // LANES), fillers)

    a = ca_scr[...]
    mu = jnp.mean(a, axis=-1, keepdims=True)
    ac = a - mu
    var = jnp.mean(ac * ac, axis=-1, keepdims=True)
    a = jax.nn.silu(ac * lax.rsqrt(var + EPS) * lng_ref[...] + lnb_ref[...])
    y_a = _dot(a.astype(BF16), wao_ref[:, :D_MODEL])
    merged = jax.nn.sigmoid(zq_scr[:, col(1)]) * y_a

    _store_conv_input(sb_scr, zb_scr[:, col(2)] * zb_scr[:, col(0)], first, last)
    b = cbw_ref[0:1, :] * sb_scr[pl.ds(HALO - 1, tm), :]
    for t in range(1, CONV_B_WIDTH):
        b = b + cbw_ref[t:t + 1, :] * sb_scr[pl.ds(HALO - 1 + t, tm), :]
    y_b = _dot((zb_scr[main, col(1)] * b).astype(BF16), wbo_ref[:, :D_MODEL])
    merged = merged + jax.nn.sigmoid(zq_scr[:, col(2)]) * y_b

    heads = []
    for h in range(N_HEADS):
        hs = slice(h * HEAD_DIM, (h + 1) * HEAD_DIM)
        s = lax.dot_general(zq_scr[:, hs].astype(BF16), k_ref[:, hs],
                            (((1,), (1,)), ((), ())), preferred_element_type=F32)
        p = jnp.exp(s - jnp.max(s, axis=-1, keepdims=True))
        denom = jnp.sum(p, axis=-1, keepdims=True)
        heads.append(_dot(p.astype(BF16), v_ref[:, hs]) / denom)
    o = jnp.concatenate(heads, axis=-1).astype(BF16)
    y_c = _dot(o, wxo_ref[:, :D_MODEL])
    merged = merged + jax.nn.sigmoid(zq_scr[:, col(3)]) * y_c

    o_ref[...] = xm_ref[...] + _dot(merged.astype(BF16), wo_ref[:, :D_MODEL])


def _mixer_call(x, u, k, v, seq_len, layer, *params):
    rows = x.shape[0]
    tm = MIX_TILE
    assert x.shape[1] == D_MODEL and u.shape == x.shape
    assert seq_len % tm == 0 and rows % seq_len == 0 and tm % HALO == 0
    tiles_per_seq = seq_len // tm
    halo_per_tile = tm // HALO
    last_halo_block = rows // HALO - 1
    ext = tm + 2 * HALO

    tile = pl.BlockSpec((tm, D_MODEL), lambda i: (i, 0))
    prev = pl.BlockSpec((HALO, D_MODEL),
                        lambda i: (jnp.maximum(i * halo_per_tile - 1, 0), 0))
    nxt = pl.BlockSpec((HALO, D_MODEL),
                       lambda i: (jnp.minimum((i + 1) * halo_per_tile, last_halo_block), 0))
    kv_blk = pl.BlockSpec((None, N_MEM, D_MODEL), lambda i: (i // tiles_per_seq, 0, 0))
    return pl.pallas_call(
        functools.partial(_mixer_kernel, tiles_per_seq=tiles_per_seq),
        grid=(rows // tm,),
        in_specs=[tile, tile, prev, nxt, kv_blk, kv_blk] + [_layer_block(w, layer) for w in params],
        out_specs=tile,
        out_shape=jax.ShapeDtypeStruct((rows, D_MODEL), F32),
        scratch_shapes=[pltpu.VMEM((ext, D_MODEL), BF16),
                        pltpu.VMEM((ext, D_MODEL), F32),
                        pltpu.VMEM((ext, D_MODEL), F32),
                        pltpu.VMEM((SUBLANES * CONV_A_WIDTH, D_MODEL), F32),
                        pltpu.VMEM((tm, D_MODEL), F32),
                        pltpu.VMEM((ext, 3 * D_MODEL), F32),
                        pltpu.VMEM((tm, 4 * D_MODEL), F32)],
        compiler_params=pltpu.CompilerParams(
            dimension_semantics=("arbitrary",), vmem_limit_bytes=VMEM_LIMIT_BYTES),
        name="mixer",
    )(x, u, u, u, k, v, *params)


_MIXER_PARAMS = ("w_in", "conv_a_w", "conv_a_b", "ln_a_g", "ln_a_b", "w_a_out",
                 "conv_b_w", "w_b_out", "w_x_out", "w_o")


def _trunk(x, mem, p, final_norm):
    b, s, _ = x.shape
    x2 = x.reshape(b * s, D_MODEL)
    depth = p["w_in"].shape[0]
    for l in range(depth):
        k, v = _kv_call(mem, p["mem_norm"], p["w_kv"], l)
        x2, u = _ffn_call(x2, l, p["ffn1_norm"], p["ffn1_wg"], p["ffn1_wu"], p["ffn1_wd"],
                          post="emit", post_g=p["mix_norm"], post_layer=l)
        x2 = _mixer_call(x2, u, k, v, s, l, *(p[name] for name in _MIXER_PARAMS))
        last = l == depth - 1
        x2 = _ffn_call(x2, l, p["ffn2_norm"], p["ffn2_wg"], p["ffn2_wu"], p["ffn2_wd"],
                       post="final" if last else None, post_g=final_norm if last else None)
    return x2.reshape(b, s, D_MODEL)


def _mxu_weight(w):
    if w.shape[-1] % ROW_PAIR_LAYOUT_COLUMNS == 0:
        w = jnp.pad(w, ((0, 0),) * (w.ndim - 1) + ((0, LANES),))
    return w.astype(BF16)


_MATMUL_WEIGHTS = ("ffn1_wg", "ffn1_wu", "ffn1_wd", "w_in", "w_a_out", "w_b_out",
                   "w_kv", "w_x_out", "w_o", "ffn2_wg", "ffn2_wu", "ffn2_wd")


def kernel(x_prompt, x_sample, mem_prompt, mem_sample, ffn1_norm, ffn1_wg, ffn1_wu, ffn1_wd, mix_norm, mem_norm, w_in, conv_a_w, conv_a_b, ln_a_g, ln_a_b, w_a_out, conv_b_w, w_b_out, w_kv, w_x_out, w_o, ffn2_norm, ffn2_wg, ffn2_wu, ffn2_wd, final_norm):
    p = dict(ffn1_norm=ffn1_norm, ffn1_wg=ffn1_wg, ffn1_wu=ffn1_wu, ffn1_wd=ffn1_wd,
             mix_norm=mix_norm, mem_norm=mem_norm, w_in=w_in, conv_a_w=conv_a_w,
             conv_a_b=conv_a_b, ln_a_g=ln_a_g, ln_a_b=ln_a_b, w_a_out=w_a_out,
             conv_b_w=conv_b_w, w_b_out=w_b_out, w_kv=w_kv, w_x_out=w_x_out, w_o=w_o,
             ffn2_norm=ffn2_norm, ffn2_wg=ffn2_wg, ffn2_wu=ffn2_wu, ffn2_wd=ffn2_wd)
    for name, w in p.items():
        p[name] = _mxu_weight(w) if name in _MATMUL_WEIGHTS else (
            w[:, None, :] if w.ndim == 2 else w)
    final_norm = final_norm.reshape(1, 1, D_MODEL)
    y_prompt = _trunk(x_prompt, mem_prompt, p, final_norm)
    y_sample = _trunk(x_sample, mem_sample, p, final_norm)
    return (y_prompt, y_sample)
```

```python
import functools

import jax
import jax.numpy as jnp
from jax import lax
from jax.experimental import pallas as pl
from jax.experimental.pallas import tpu as pltpu

D_MODEL = 1024
N_MEM = 256
N_HEADS = 4
HEAD_DIM = D_MODEL // N_HEADS
CONV_A_WIDTH = 31
CONV_B_WIDTH = 3
EPS = 1e-6

HALO = 16
SUBLANES = 8
LANES = 128
ROW_PAIR_LAYOUT_COLUMNS = 8 * LANES
FFN_TILE = 1024
FFN_SUBTILE = 512
MIX_TILE = 512
PROJ_CHUNK = 512
FILLER_FIRST_STEP = 12
FILLER_PERIOD = 44
VMEM_LIMIT_BYTES = 56 * 1024 * 1024

F32 = jnp.float32
BF16 = jnp.bfloat16


def _rmsnorm(x, g):
    return x * lax.rsqrt(jnp.mean(x * x, axis=-1, keepdims=True) + EPS) * g


def _dot(a, b):
    return jnp.dot(a, b, preferred_element_type=F32)


def _layer_block(stacked, layer):
    shape = stacked.shape[1:]
    return pl.BlockSpec((None,) + shape, lambda i: (layer,) + (0,) * len(shape),
                        pipeline_mode=pl.Buffered(1))


def _kv_kernel(mem_ref, g_ref, wkv_ref, k_ref, v_ref):
    mem_n = _rmsnorm(mem_ref[...], g_ref[...]).astype(BF16)
    kv = _dot(mem_n, wkv_ref[:, :2 * D_MODEL])
    k_ref[...] = (kv[:, :D_MODEL] * (HEAD_DIM ** -0.5)).T.astype(BF16)
    v_ref[...] = kv[:, D_MODEL:].astype(BF16)


def _kv_call(mem, g, wkv, layer):
    b = mem.shape[0]
    out = jax.ShapeDtypeStruct((b, N_MEM, D_MODEL), BF16)
    out_t = jax.ShapeDtypeStruct((b, D_MODEL, N_MEM), BF16)
    blk = pl.BlockSpec((None, N_MEM, D_MODEL), lambda i: (i, 0, 0))
    blk_t = pl.BlockSpec((None, D_MODEL, N_MEM), lambda i: (i, 0, 0))
    return pl.pallas_call(
        _kv_kernel,
        grid=(b,),
        in_specs=[blk, _layer_block(g, layer), _layer_block(wkv, layer)],
        out_specs=[blk_t, blk],
        out_shape=[out_t, out],
        compiler_params=pltpu.CompilerParams(
            dimension_semantics=("arbitrary",), vmem_limit_bytes=VMEM_LIMIT_BYTES),
        name="kv_proj",
    )(mem, g, wkv)


def _ffn_kernel(x_ref, g_ref, wg_ref, wu_ref, wd_ref, *rest, post):
    post_g_ref, outs = (rest[0], rest[1:]) if post else (None, rest)
    for r in range(0, x_ref.shape[0], FFN_SUBTILE):
        rows = slice(r, r + FFN_SUBTILE)
        x = x_ref[rows, :]
        h = _rmsnorm(x, g_ref[...]).astype(BF16)
        gate = _dot(h, wg_ref[...])
        up = _dot(h, wu_ref[...])
        act = (jax.nn.silu(gate) * up).astype(BF16)
        y = x + 0.5 * _dot(act, wd_ref[:, :D_MODEL])
        if post == "final":
            outs[0][rows, :] = _rmsnorm(y, post_g_ref[...])
        else:
            outs[0][rows, :] = y
            if post == "emit":
                outs[1][rows, :] = _rmsnorm(y, post_g_ref[...]).astype(BF16)


def _ffn_call(x, layer, g, wg, wu, wd, post=None, post_g=None, post_layer=0):
    rows = x.shape[0]
    assert x.shape[1] == D_MODEL and rows % FFN_TILE == 0 and FFN_TILE % FFN_SUBTILE == 0
    tile = pl.BlockSpec((FFN_TILE, D_MODEL), lambda i: (i, 0))
    in_specs = [tile] + [_layer_block(w, layer) for w in (g, wg, wu, wd)]
    args = [x, g, wg, wu, wd]
    if post:
        in_specs.append(_layer_block(post_g, post_layer))
        args.append(post_g)
    y_shape = jax.ShapeDtypeStruct((rows, D_MODEL), F32)
    emit = post == "emit"
    return pl.pallas_call(
        functools.partial(_ffn_kernel, post=post),
        grid=(rows // FFN_TILE,),
        in_specs=in_specs,
        out_specs=[tile, tile] if emit else tile,
        out_shape=[y_shape, jax.ShapeDtypeStruct((rows, D_MODEL), BF16)] if emit else y_shape,
        compiler_params=pltpu.CompilerParams(
            dimension_semantics=("arbitrary",), vmem_limit_bytes=VMEM_LIMIT_BYTES),
        name="ffn_" + post if post else "ffn",
    )(*args)


def _conv_a_blocks(s_scr, wb_scr, bias, out_scr):
    tm = out_scr.shape[0]
    first = HALO - CONV_A_WIDTH // 2
    nq = pl.cdiv(first + CONV_A_WIDTH, SUBLANES)
    sub = lax.broadcasted_iota(jnp.int32, (SUBLANES, LANES), 0)

    for c in range(0, D_MODEL, LANES):
        lanes = slice(c, c + LANES)
        w = [wb_scr[SUBLANES * k:SUBLANES * (k + 1), lanes] for k in range(CONV_A_WIDTH)]
        bias_c = bias[:, lanes]

        def rotated_partials(j):
            s = [s_scr[j + SUBLANES * q:j + SUBLANES * (q + 1), lanes] for q in range(nq)]
            parts = []
            for r in range(SUBLANES):
                acc = None
                for q in range(nq):
                    k = SUBLANES * q + r - first
                    if 0 <= k < CONV_A_WIDTH:
                        term = w[k] * s[q]
                        acc = term if acc is None else acc + term
                parts.append(acc if r == 0 else pltpu.roll(acc, SUBLANES - r, axis=0))
            return parts

        cur = rotated_partials(0)
        for blk in range(tm // SUBLANES):
            nxt = rotated_partials(SUBLANES * (blk + 1))
            acc = cur[0] + bias_c
            for r in range(1, SUBLANES):
                acc = acc + jnp.where(sub < SUBLANES - r, cur[r], nxt[r])
            after = yield
            if after is not None:
                acc = acc + after
            out_scr[SUBLANES * blk:SUBLANES * (blk + 1), lanes] = acc
            cur = nxt


def _ordering_zero(x):
    bits = x[-SUBLANES:, -LANES:].astype(jnp.int32)
    return lax.shift_right_logical(lax.shift_right_logical(bits, 16), 16).astype(F32)


def _store_conv_input(scr, x, first, last):
    body = scr.shape[0] - HALO
    scr[0:HALO, :] = jnp.where(first, 0.0, x[0:HALO, :])
    scr[HALO:body, :] = x[HALO:body, :]
    scr[body:, :] = jnp.where(last, 0.0, x[body:, :])


def _interleave(steps, n_steps, fillers):
    zeros = [filler() for filler in fillers]
    next(steps)
    for j in range(n_steps):
        k = (j - FILLER_FIRST_STEP) // FILLER_PERIOD
        after = zeros[k] if 0 <= k < len(zeros) else None
        try:
            steps.send(after)
        except StopIteration:
            assert j == n_steps - 1


def _mixer_kernel(xm_ref, um_ref, up_ref, un_ref, k_ref, v_ref, win_ref,
                  caw_ref, cab_ref, lng_ref, lnb_ref, wao_ref, cbw_ref, wbo_ref,
                  wxo_ref, wo_ref, o_ref, u_scr, sa_scr, sb_scr, wb_scr, ca_scr,
                  zb_scr, zq_scr, *, tiles_per_seq):
    tm = xm_ref.shape[0]
    ext = tm + 2 * HALO
    main = slice(HALO, HALO + tm)
    col = lambda j: slice(j * D_MODEL, (j + 1) * D_MODEL)

    i = pl.program_id(0)
    first = (i % tiles_per_seq) == 0
    last = (i % tiles_per_seq) == tiles_per_seq - 1

    u_scr[0:HALO, :] = up_ref[...]
    u_scr[main, :] = um_ref[...]
    u_scr[HALO + tm:ext, :] = un_ref[...]

    za = _dot(u_scr[...], win_ref[:, 0:2 * D_MODEL])
    _store_conv_input(sa_scr, za[:, col(0)] * jax.nn.sigmoid(za[:, col(1)]), first, last)
    for t in range(CONV_A_WIDTH):
        wb_scr[SUBLANES * t:SUBLANES * (t + 1), :] = jnp.broadcast_to(
            caw_ref[t:t + 1, :], (SUBLANES, D_MODEL))

    def project_b(c):
        z = _dot(u_scr[...], win_ref[:, 2 * D_MODEL + c:2 * D_MODEL + c + PROJ_CHUNK])
        zb_scr[:, c:c + PROJ_CHUNK] = z
        return _ordering_zero(z)

    def project_q(c):
        z = _dot(u_scr[main, :], win_ref[:, 5 * D_MODEL + c:5 * D_MODEL + c + PROJ_CHUNK])
        zq_scr[:, c:c + PROJ_CHUNK] = z
        return _ordering_zero(z)

    fillers = [functools.partial(project_b, c) for c in range(0, 3 * D_MODEL, PROJ_CHUNK)]
    fillers += [functools.partial(project_q, c) for c in range(0, 4 * D_MODEL, PROJ_CHUNK)]
    conv = _conv_a_blocks(sa_scr, wb_scr,
                          jnp.broadcast_to(cab_ref[...], (SUBLANES, D_MODEL)), ca_scr)
    _interleave(conv, (tm // SUBLANES) * (D_MODEL // LANES), fillers)

    a = ca_scr[...]
    mu = jnp.mean(a, axis=-1, keepdims=True)
    ac = a - mu
    var = jnp.mean(ac * ac, axis=-1, keepdims=True)
    a = jax.nn.silu(ac * lax.rsqrt(var + EPS) * lng_ref[...] + lnb_ref[...])
    y_a = _dot(a.astype(BF16), wao_ref[:, :D_MODEL])
    merged = jax.nn.sigmoid(zq_scr[:, col(1)]) * y_a

    _store_conv_input(sb_scr, zb_scr[:, col(2)] * zb_scr[:, col(0)], first, last)
    b = cbw_ref[0:1, :] * sb_scr[pl.ds(HALO - 1, tm), :]
    for t in range(1, CONV_B_WIDTH):
        b = b + cbw_ref[t:t + 1, :] * sb_scr[pl.ds(HALO - 1 + t, tm), :]
    y_b = _dot((zb_scr[main, col(1)] * b).astype(BF16), wbo_ref[:, :D_MODEL])
    merged = merged + jax.nn.sigmoid(zq_scr[:, col(2)]) * y_b

    heads = []
    for h in range(N_HEADS):
        hs = slice(h * HEAD_DIM, (h + 1) * HEAD_DIM)
        s = _dot(zq_scr[:, hs].astype(BF16), k_ref[hs, :])
        p = jnp.exp(s - jnp.max(s, axis=-1, keepdims=True))
        denom = jnp.sum(p, axis=-1, keepdims=True)
        heads.append(_dot(p.astype(BF16), v_ref[:, hs]) / denom)
    o = jnp.concatenate(heads, axis=-1).astype(BF16)
    y_c = _dot(o, wxo_ref[:, :D_MODEL])
    merged = merged + jax.nn.sigmoid(zq_scr[:, col(3)]) * y_c

    o_ref[...] = xm_ref[...] + _dot(merged.astype(BF16), wo_ref[:, :D_MODEL])


def _mixer_call(x, u, k, v, seq_len, layer, *params):
    rows = x.shape[0]
    tm = MIX_TILE
    assert x.shape[1] == D_MODEL and u.shape == x.shape
    assert seq_len % tm == 0 and rows % seq_len == 0 and tm % HALO == 0
    tiles_per_seq = seq_len // tm
    halo_per_tile = tm // HALO
    last_halo_block = rows // HALO - 1
    ext = tm + 2 * HALO

    tile = pl.BlockSpec((tm, D_MODEL), lambda i: (i, 0))
    prev = pl.BlockSpec((HALO, D_MODEL),
                        lambda i: (jnp.maximum(i * halo_per_tile - 1, 0), 0))
    nxt = pl.BlockSpec((HALO, D_MODEL),
                       lambda i: (jnp.minimum((i + 1) * halo_per_tile, last_halo_block), 0))
    kv_blk = pl.BlockSpec((None, N_MEM, D_MODEL), lambda i: (i // tiles_per_seq, 0, 0))
    kt_blk = pl.BlockSpec((None, D_MODEL, N_MEM), lambda i: (i // tiles_per_seq, 0, 0))
    return pl.pallas_call(
        functools.partial(_mixer_kernel, tiles_per_seq=tiles_per_seq),
        grid=(rows // tm,),
        in_specs=[tile, tile, prev, nxt, kt_blk, kv_blk] + [_layer_block(w, layer) for w in params],
        out_specs=tile,
        out_shape=jax.ShapeDtypeStruct((rows, D_MODEL), F32),
        scratch_shapes=[pltpu.VMEM((ext, D_MODEL), BF16),
                        pltpu.VMEM((ext, D_MODEL), F32),
                        pltpu.VMEM((ext, D_MODEL), F32),
                        pltpu.VMEM((SUBLANES * CONV_A_WIDTH, D_MODEL), F32),
                        pltpu.VMEM((tm, D_MODEL), F32),
                        pltpu.VMEM((ext, 3 * D_MODEL), F32),
                        pltpu.VMEM((tm, 4 * D_MODEL), F32)],
        compiler_params=pltpu.CompilerParams(
            dimension_semantics=("arbitrary",), vmem_limit_bytes=VMEM_LIMIT_BYTES),
        name="mixer",
    )(x, u, u, u, k, v, *params)


_MIXER_PARAMS = ("w_in", "conv_a_w", "conv_a_b", "ln_a_g", "ln_a_b", "w_a_out",
                 "conv_b_w", "w_b_out", "w_x_out", "w_o")


def _trunk(x, mem, p, final_norm):
    b, s, _ = x.shape
    x2 = x.reshape(b * s, D_MODEL)
    depth = p["w_in"].shape[0]
    for l in range(depth):
        k, v = _kv_call(mem, p["mem_norm"], p["w_kv"], l)
        x2, u = _ffn_call(x2, l, p["ffn1_norm"], p["ffn1_wg"], p["ffn1_wu"], p["ffn1_wd"],
                          post="emit", post_g=p["mix_norm"], post_layer=l)
        x2 = _mixer_call(x2, u, k, v, s, l, *(p[name] for name in _MIXER_PARAMS))
        last = l == depth - 1
        x2 = _ffn_call(x2, l, p["ffn2_norm"], p["ffn2_wg"], p["ffn2_wu"], p["ffn2_wd"],
                       post="final" if last else None, post_g=final_norm if last else None)
    return x2.reshape(b, s, D_MODEL)


def _mxu_weight(w):
    if w.shape[-1] % ROW_PAIR_LAYOUT_COLUMNS == 0:
        w = jnp.pad(w, ((0, 0),) * (w.ndim - 1) + ((0, LANES),))
    return w.astype(BF16)


_MATMUL_WEIGHTS = ("ffn1_wg", "ffn1_wu", "ffn1_wd", "w_in", "w_a_out", "w_b_out",
                   "w_kv", "w_x_out", "w_o", "ffn2_wg", "ffn2_wu", "ffn2_wd")


def kernel(x_prompt, x_sample, mem_prompt, mem_sample, ffn1_norm, ffn1_wg, ffn1_wu, ffn1_wd, mix_norm, mem_norm, w_in, conv_a_w, conv_a_b, ln_a_g, ln_a_b, w_a_out, conv_b_w, w_b_out, w_kv, w_x_out, w_o, ffn2_norm, ffn2_wg, ffn2_wu, ffn2_wd, final_norm):
    p = dict(ffn1_norm=ffn1_norm, ffn1_wg=ffn1_wg, ffn1_wu=ffn1_wu, ffn1_wd=ffn1_wd,
             mix_norm=mix_norm, mem_norm=mem_norm, w_in=w_in, conv_a_w=conv_a_w,
             conv_a_b=conv_a_b, ln_a_g=ln_a_g, ln_a_b=ln_a_b, w_a_out=w_a_out,
             conv_b_w=conv_b_w, w_b_out=w_b_out, w_kv=w_kv, w_x_out=w_x_out, w_o=w_o,
             ffn2_norm=ffn2_norm, ffn2_wg=ffn2_wg, ffn2_wu=ffn2_wu, ffn2_wd=ffn2_wd)
    for name, w in p.items():
        p[name] = _mxu_weight(w) if name in _MATMUL_WEIGHTS else (
            w[:, None, :] if w.ndim == 2 else w)
    final_norm = final_norm.reshape(1, 1, D_MODEL)
    y_prompt = _trunk(x_prompt, mem_prompt, p, final_norm)
    y_sample = _trunk(x_sample, mem_sample, p, final_norm)
    return (y_prompt, y_sample)
```

```python
import functools

import jax
import jax.numpy as jnp
from jax import lax
from jax.experimental import pallas as pl
from jax.experimental.pallas import tpu as pltpu

D_MODEL = 1024
N_MEM = 256
N_HEADS = 4
HEAD_DIM = D_MODEL // N_HEADS
CONV_A_WIDTH = 31
CONV_B_WIDTH = 3
EPS = 1e-6

HALO = 16
SUBLANES = 8
LANES = 128
ROW_PAIR_LAYOUT_COLUMNS = 8 * LANES
FFN_TILE = 1024
FFN_SUBTILE = 512
MIX_TILE = 512
PROJ_CHUNK = 512
FILLER_FIRST_STEP = 12
FILLER_PERIOD = 44
VMEM_LIMIT_BYTES = 56 * 1024 * 1024

F32 = jnp.float32
BF16 = jnp.bfloat16


def _rmsnorm(x, g):
    return x * lax.rsqrt(jnp.mean(x * x, axis=-1, keepdims=True) + EPS) * g


def _dot(a, b):
    return jnp.dot(a, b, preferred_element_type=F32)


def _layer_block(stacked, layer):
    shape = stacked.shape[1:]
    return pl.BlockSpec((None,) + shape, lambda i: (layer,) + (0,) * len(shape),
                        pipeline_mode=pl.Buffered(1))


def _kv_kernel(mem_ref, g_ref, wkv_ref, k_ref, v_ref):
    mem_n = _rmsnorm(mem_ref[...], g_ref[...]).astype(BF16)
    kv = _dot(mem_n, wkv_ref[:, :2 * D_MODEL])
    k_ref[...] = (kv[:, :D_MODEL] * (HEAD_DIM ** -0.5)).astype(BF16)
    v_ref[...] = kv[:, D_MODEL:].astype(BF16)


def _kv_call(mem, g, wkv, layer):
    b = mem.shape[0]
    out = jax.ShapeDtypeStruct((b, N_MEM, D_MODEL), BF16)
    blk = pl.BlockSpec((None, N_MEM, D_MODEL), lambda i: (i, 0, 0))
    return pl.pallas_call(
        _kv_kernel,
        grid=(b,),
        in_specs=[blk, _layer_block(g, layer), _layer_block(wkv, layer)],
        out_specs=[blk, blk],
        out_shape=[out, out],
        compiler_params=pltpu.CompilerParams(
            dimension_semantics=("parallel",), vmem_limit_bytes=VMEM_LIMIT_BYTES),
        name="kv_proj",
    )(mem, g, wkv)


def _ffn_kernel(x_ref, g_ref, wg_ref, wu_ref, wd_ref, *rest, post):
    post_g_ref, outs = (rest[0], rest[1:]) if post else (None, rest)
    for r in range(0, x_ref.shape[0], FFN_SUBTILE):
        rows = slice(r, r + FFN_SUBTILE)
        x = x_ref[rows, :]
        h = _rmsnorm(x, g_ref[...]).astype(BF16)
        gate = _dot(h, wg_ref[...])
        up = _dot(h, wu_ref[...])
        act = (jax.nn.silu(gate) * up).astype(BF16)
        y = x + 0.5 * _dot(act, wd_ref[:, :D_MODEL])
        if post == "final":
            outs[0][rows, :] = _rmsnorm(y, post_g_ref[...])
        else:
            outs[0][rows, :] = y
            if post == "emit":
                outs[1][rows, :] = _rmsnorm(y, post_g_ref[...]).astype(BF16)


def _ffn_call(x, layer, g, wg, wu, wd, post=None, post_g=None, post_layer=0):
    rows = x.shape[0]
    assert x.shape[1] == D_MODEL and rows % FFN_TILE == 0 and FFN_TILE % FFN_SUBTILE == 0
    tile = pl.BlockSpec((FFN_TILE, D_MODEL), lambda i: (i, 0))
    in_specs = [tile] + [_layer_block(w, layer) for w in (g, wg, wu, wd)]
    args = [x, g, wg, wu, wd]
    if post:
        in_specs.append(_layer_block(post_g, post_layer))
        args.append(post_g)
    y_shape = jax.ShapeDtypeStruct((rows, D_MODEL), F32)
    emit = post == "emit"
    return pl.pallas_call(
        functools.partial(_ffn_kernel, post=post),
        grid=(rows // FFN_TILE,),
        in_specs=in_specs,
        out_specs=[tile, tile] if emit else tile,
        out_shape=[y_shape, jax.ShapeDtypeStruct((rows, D_MODEL), BF16)] if emit else y_shape,
        compiler_params=pltpu.CompilerParams(
            dimension_semantics=("parallel",), vmem_limit_bytes=VMEM_LIMIT_BYTES),
        name="ffn_" + post if post else "ffn",
    )(*args)


def _conv_a_blocks(s_scr, wb_scr, bias, out_scr):
    tm = out_scr.shape[0]
    first = HALO - CONV_A_WIDTH // 2
    nq = pl.cdiv(first + CONV_A_WIDTH, SUBLANES)
    sub = lax.broadcasted_iota(jnp.int32, (SUBLANES, LANES), 0)

    for c in range(0, D_MODEL, LANES):
        lanes = slice(c, c + LANES)
        w = [wb_scr[SUBLANES * k:SUBLANES * (k + 1), lanes] for k in range(CONV_A_WIDTH)]
        bias_c = bias[:, lanes]

        def rotated_partials(j):
            s = [s_scr[j + SUBLANES * q:j + SUBLANES * (q + 1), lanes] for q in range(nq)]
            parts = []
            for r in range(SUBLANES):
                acc = None
                for q in range(nq):
                    k = SUBLANES * q + r - first
                    if 0 <= k < CONV_A_WIDTH:
                        term = w[k] * s[q]
                        acc = term if acc is None else acc + term
                parts.append(acc if r == 0 else pltpu.roll(acc, SUBLANES - r, axis=0))
            return parts

        cur = rotated_partials(0)
        for blk in range(tm // SUBLANES):
            nxt = rotated_partials(SUBLANES * (blk + 1))
            acc = cur[0] + bias_c
            for r in range(1, SUBLANES):
                acc = acc + jnp.where(sub < SUBLANES - r, cur[r], nxt[r])
            after = yield
            if after is not None:
                acc = acc + after
            out_scr[SUBLANES * blk:SUBLANES * (blk + 1), lanes] = acc
            cur = nxt


def _ordering_zero(x):
    bits = x[-SUBLANES:, -LANES:].astype(jnp.int32)
    return lax.shift_right_logical(lax.shift_right_logical(bits, 16), 16).astype(F32)


def _store_conv_input(scr, x, first, last):
    body = scr.shape[0] - HALO
    scr[0:HALO, :] = jnp.where(first, 0.0, x[0:HALO, :])
    scr[HALO:body, :] = x[HALO:body, :]
    scr[body:, :] = jnp.where(last, 0.0, x[body:, :])


def _interleave(steps, n_steps, fillers):
    zeros = [filler() for filler in fillers]
    next(steps)
    for j in range(n_steps):
        k = (j - FILLER_FIRST_STEP) // FILLER_PERIOD
        after = zeros[k] if 0 <= k < len(zeros) else None
        try:
            steps.send(after)
        except StopIteration:
            assert j == n_steps - 1


def _mixer_kernel(xm_ref, um_ref, up_ref, un_ref, k_ref, v_ref, win_ref,
                  caw_ref, cab_ref, lng_ref, lnb_ref, wao_ref, cbw_ref, wbo_ref,
                  wxo_ref, wo_ref, o_ref, u_scr, sa_scr, sb_scr, wb_scr, ca_scr,
                  zb_scr, zq_scr, *, tiles_per_seq):
    tm = xm_ref.shape[0]
    ext = tm + 2 * HALO
    main = slice(HALO, HALO + tm)
    col = lambda j: slice(j * D_MODEL, (j + 1) * D_MODEL)

    i = pl.program_id(0)
    first = (i % tiles_per_seq) == 0
    last = (i % tiles_per_seq) == tiles_per_seq - 1

    u_scr[0:HALO, :] = up_ref[...]
    u_scr[main, :] = um_ref[...]
    u_scr[HALO + tm:ext, :] = un_ref[...]

    za = _dot(u_scr[...], win_ref[:, 0:2 * D_MODEL])
    _store_conv_input(sa_scr, za[:, col(0)] * jax.nn.sigmoid(za[:, col(1)]), first, last)
    for t in range(CONV_A_WIDTH):
        wb_scr[SUBLANES * t:SUBLANES * (t + 1), :] = jnp.broadcast_to(
            caw_ref[t:t + 1, :], (SUBLANES, D_MODEL))

    def project_b(c):
        z = _dot(u_scr[...], win_ref[:, 2 * D_MODEL + c:2 * D_MODEL + c + PROJ_CHUNK])
        zb_scr[:, c:c + PROJ_CHUNK] = z
        return _ordering_zero(z)

    def project_q(c):
        z = _dot(u_scr[main, :], win_ref[:, 5 * D_MODEL + c:5 * D_MODEL + c + PROJ_CHUNK])
        zq_scr[:, c:c + PROJ_CHUNK] = z
        return _ordering_zero(z)

    fillers = [functools.partial(project_b, c) for c in range(0, 3 * D_MODEL, PROJ_CHUNK)]
    fillers += [functools.partial(project_q, c) for c in range(0, 4 * D_MODEL, PROJ_CHUNK)]
    conv = _conv_a_blocks(sa_scr, wb_scr,
                          jnp.broadcast_to(cab_ref[...], (SUBLANES, D_MODEL)), ca_scr)
    _interleave(conv, (tm // SUBLANES) * (D_MODEL // LANES), fillers)

    a = ca_scr[...]
    mu = jnp.mean(a, axis=-1, keepdims=True)
    ac = a - mu
    var = jnp.mean(ac * ac, axis=-1, keepdims=True)
    a = jax.nn.silu(ac * lax.rsqrt(var + EPS) * lng_ref[...] + lnb_ref[...])
    y_a = _dot(a.astype(BF16), wao_ref[:, :D_MODEL])
    merged = jax.nn.sigmoid(zq_scr[:, col(1)]) * y_a

    _store_conv_input(sb_scr, zb_scr[:, col(2)] * zb_scr[:, col(0)], first, last)
    b = cbw_ref[0:1, :] * sb_scr[pl.ds(HALO - 1, tm), :]
    for t in range(1, CONV_B_WIDTH):
        b = b + cbw_ref[t:t + 1, :] * sb_scr[pl.ds(HALO - 1 + t, tm), :]
    y_b = _dot((zb_scr[main, col(1)] * b).astype(BF16), wbo_ref[:, :D_MODEL])
    merged = merged + jax.nn.sigmoid(zq_scr[:, col(2)]) * y_b

    heads = []
    for h in range(N_HEADS):
        hs = slice(h * HEAD_DIM, (h + 1) * HEAD_DIM)
        s = lax.dot_general(zq_scr[:, hs].astype(BF16), k_ref[:, hs],
                            (((1,), (1,)), ((), ())), preferred_element_type=F32)
        p = jnp.exp(s - jnp.max(s, axis=-1, keepdims=True))
        denom = jnp.sum(p, axis=-1, keepdims=True)
        heads.append(_dot(p.astype(BF16), v_ref[:, hs]) / denom)
    o = jnp.concatenate(heads, axis=-1).astype(BF16)
    y_c = _dot(o, wxo_ref[:, :D_MODEL])
    merged = merged + jax.nn.sigmoid(zq_scr[:, col(3)]) * y_c

    o_ref[...] = xm_ref[...] + _dot(merged.astype(BF16), wo_ref[:, :D_MODEL])


def _mixer_call(x, u, k, v, seq_len, layer, *params):
    rows = x.shape[0]
    tm = MIX_TILE
    assert x.shape[1] == D_MODEL and u.shape == x.shape
    assert seq_len % tm == 0 and rows % seq_len == 0 and tm % HALO == 0
    tiles_per_seq = seq_len // tm
    halo_per_tile = tm // HALO
    last_halo_block = rows // HALO - 1
    ext = tm + 2 * HALO

    tile = pl.BlockSpec((tm, D_MODEL), lambda i: (i, 0))
    prev = pl.BlockSpec((HALO, D_MODEL),
                        lambda i: (jnp.maximum(i * halo_per_tile - 1, 0), 0))
    nxt = pl.BlockSpec((HALO, D_MODEL),
                       lambda i: (jnp.minimum((i + 1) * halo_per_tile, last_halo_block), 0))
    kv_blk = pl.BlockSpec((None, N_MEM, D_MODEL), lambda i: (i // tiles_per_seq, 0, 0))
    return pl.pallas_call(
        functools.partial(_mixer_kernel, tiles_per_seq=tiles_per_seq),
        grid=(rows // tm,),
        in_specs=[tile, tile, prev, nxt, kv_blk, kv_blk] + [_layer_block(w, layer) for w in params],
        out_specs=tile,
        out_shape=jax.ShapeDtypeStruct((rows, D_MODEL), F32),
        scratch_shapes=[pltpu.VMEM((ext, D_MODEL), BF16),
                        pltpu.VMEM((ext, D_MODEL), F32),
                        pltpu.VMEM((ext, D_MODEL), F32),
                        pltpu.VMEM((SUBLANES * CONV_A_WIDTH, D_MODEL), F32),
                        pltpu.VMEM((tm, D_MODEL), F32),
                        pltpu.VMEM((ext, 3 * D_MODEL), F32),
                        pltpu.VMEM((tm, 4 * D_MODEL), F32)],
        compiler_params=pltpu.CompilerParams(
            dimension_semantics=("parallel",), vmem_limit_bytes=VMEM_LIMIT_BYTES),
        name="mixer",
    )(x, u, u, u, k, v, *params)


_MIXER_PARAMS = ("w_in", "conv_a_w", "conv_a_b", "ln_a_g", "ln_a_b", "w_a_out",
                 "conv_b_w", "w_b_out", "w_x_out", "w_o")


def _trunk(x, mem, p, final_norm):
    b, s, _ = x.shape
    x2 = x.reshape(b * s, D_MODEL)
    depth = p["w_in"].shape[0]
    for l in range(depth):
        k, v = _kv_call(mem, p["mem_norm"], p["w_kv"], l)
        x2, u = _ffn_call(x2, l, p["ffn1_norm"], p["ffn1_wg"], p["ffn1_wu"], p["ffn1_wd"],
                          post="emit", post_g=p["mix_norm"], post_layer=l)
        x2 = _mixer_call(x2, u, k, v, s, l, *(p[name] for name in _MIXER_PARAMS))
        last = l == depth - 1
        x2 = _ffn_call(x2, l, p["ffn2_norm"], p["ffn2_wg"], p["ffn2_wu"], p["ffn2_wd"],
                       post="final" if last else None, post_g=final_norm if last else None)
    return x2.reshape(b, s, D_MODEL)


def _mxu_weight(w):
    if w.shape[-1] % ROW_PAIR_LAYOUT_COLUMNS == 0:
        w = jnp.pad(w, ((0, 0),) * (w.ndim - 1) + ((0, LANES),))
    return w.astype(BF16)


_MATMUL_WEIGHTS = ("ffn1_wg", "ffn1_wu", "ffn1_wd", "w_in", "w_a_out", "w_b_out",
                   "w_kv", "w_x_out", "w_o", "ffn2_wg", "ffn2_wu", "ffn2_wd")


def kernel(x_prompt, x_sample, mem_prompt, mem_sample, ffn1_norm, ffn1_wg, ffn1_wu, ffn1_wd, mix_norm, mem_norm, w_in, conv_a_w, conv_a_b, ln_a_g, ln_a_b, w_a_out, conv_b_w, w_b_out, w_kv, w_x_out, w_o, ffn2_norm, ffn2_wg, ffn2_wu, ffn2_wd, final_norm):
    p = dict(ffn1_norm=ffn1_norm, ffn1_wg=ffn1_wg, ffn1_wu=ffn1_wu, ffn1_wd=ffn1_wd,
             mix_norm=mix_norm, mem_norm=mem_norm, w_in=w_in, conv_a_w=conv_a_w,
             conv_a_b=conv_a_b, ln_a_g=ln_a_g, ln_a_b=ln_a_b, w_a_out=w_a_out,
             conv_b_w=conv_b_w, w_b_out=w_b_out, w_kv=w_kv, w_x_out=w_x_out, w_o=w_o,
             ffn2_norm=ffn2_norm, ffn2_wg=ffn2_wg, ffn2_wu=ffn2_wu, ffn2_wd=ffn2_wd)
    for name, w in p.items():
        p[name] = _mxu_weight(w) if name in _MATMUL_WEIGHTS else (
            w[:, None, :] if w.ndim == 2 else w)
    final_norm = final_norm.reshape(1, 1, D_MODEL)
    y_prompt = _trunk(x_prompt, mem_prompt, p, final_norm)
    y_sample = _trunk(x_sample, mem_sample, p, final_norm)
    return (y_prompt, y_sample)
```
